```python
import jax, jax.numpy as jnp
from jax import lax
import numpy as np

D_MODEL = 1024
BATCH = 8
SEQ = 2048
DEPTH = 2
DEC_BATCH = 32
DEC_SEQ = 1
PAST_LEN = 16384
PAGE_SIZE = 128

H_A = 8
NOPE_A = 64
ROPE_A = 32
V_A = 64
Q_RANK = 256
KV_RANK = 256
MLA_SCALE = (NOPE_A + ROPE_A) ** -0.5
H_B = 4
DH_B = 64
H_IDX = 4
D_IDX = 64
TOPK_MAX = 256
POOL_WINDOWS = (2, 4, 8, 16)
N_POOL_GROUPS = 4
POOL_CH = 64
C_WIDTH = N_POOL_GROUPS * POOL_CH
POOL_BUF = max(POOL_WINDOWS) - 1
A_WIDTH = H_A * V_A
B_WIDTH = H_B * DH_B
MIX_WIDTH = A_WIDTH + B_WIDTH + C_WIDTH
IN_SPLITS = (Q_RANK, KV_RANK, ROPE_A, H_B * DH_B, H_B * DH_B, H_B * DH_B, H_IDX * D_IDX, D_IDX, H_IDX, C_WIDTH)
N_IN = sum(IN_SPLITS)
D_FF = 3584
N_EXPERTS = 8
TOP_K_EXPERTS = 2
N_DENSE = (DEPTH + 1) // 2
N_MOE = DEPTH // 2
ROPE_THETA = 10000.0
Q_BLOCK = 128
ALPHA = (2 * DEPTH) ** 0.25
BETA = (8 * DEPTH) ** -0.25
LN_EPS = 1e-5
RMS_EPS = 1e-6

kernel_name = "hybrid_mla_dsa_pool_deepnorm_step"


def layer_norm(x, g, b):
    xf = x.astype(jnp.float32)
    mu = jnp.mean(xf, -1, keepdims=True)
    var = jnp.mean(jnp.square(xf - mu), -1, keepdims=True)
    return ((xf - mu) * lax.rsqrt(var + LN_EPS) * g + b).astype(x.dtype)


def rms_norm(x, g):
    xf = x.astype(jnp.float32)
    return (xf * lax.rsqrt(jnp.mean(xf * xf, -1, keepdims=True) + RMS_EPS) * g).astype(x.dtype)


def rope(x, pos):
    d = x.shape[-1]
    inv = ROPE_THETA ** (-jnp.arange(0, d, 2, dtype=jnp.float32) / d)
    ang = pos.astype(jnp.float32)[:, None] * inv[None, :]
    shape = (ang.shape[0],) + (1,) * (x.ndim - 3) + (d // 2,)
    cos, sin = jnp.cos(ang).reshape(shape), jnp.sin(ang).reshape(shape)
    xf = x.astype(jnp.float32)
    x1, x2 = xf[..., : d // 2], xf[..., d // 2:]
    return jnp.concatenate([x1 * cos - x2 * sin, x2 * cos + x1 * sin], -1).astype(x.dtype)


def _split_in(h):
    offs = [int(o) for o in np.cumsum(IN_SPLITS)[:-1]]
    return jnp.split(h, offs, axis=-1)


def _project(x, pos, p):
    b, t, _ = x.shape
    cq, ckv, kr, qb, kb, vb, qi, ki, wi, xc = _split_in(jnp.einsum('btd,dn->btn', x, p['w_in']))
    q = jnp.einsum('btr,rn->btn', rms_norm(cq, p['a_q_norm']), p['a_w_uq']).reshape(b, t, H_A, NOPE_A + ROPE_A)
    q_rope = rope(q[..., NOPE_A:], pos)
    q_lat = jnp.einsum('bthn,rhn->bthr', q[..., :NOPE_A], p['a_w_uk'])
    ckv_n = rms_norm(ckv, p['a_kv_norm'])
    kr_r = rope(kr, pos)
    qb = rope(qb.reshape(b, t, H_B, DH_B), pos)
    kb = rope(kb.reshape(b, t, H_B, DH_B), pos)
    vb = vb.reshape(b, t, H_B, DH_B)
    qi = rope(qi.reshape(b, t, H_IDX, D_IDX), pos)
    ki = rope(ki, pos)
    return q_lat, q_rope, ckv_n, kr_r, qb, kb, vb, qi, ki, wi, xc


def _mla_attend(q_lat, q_rope, ckv, kr, q_pos, k_pos):
    s = (jnp.einsum('bthr,bsr->bhts', q_lat, ckv, preferred_element_type=jnp.float32)
         + jnp.einsum('bthd,bsd->bhts', q_rope, kr, preferred_element_type=jnp.float32)) * MLA_SCALE
    s = jnp.where(k_pos[None, :] <= q_pos[:, None], s, -jnp.inf)
    p = jax.nn.softmax(s, axis=-1).astype(ckv.dtype)
    return jnp.einsum('bhts,bsr->bthr', p, ckv)


def _indexer_scores(qi, wi, ki):
    dots = jnp.einsum('bthd,bsd->bths', qi, ki, preferred_element_type=jnp.float32) * (D_IDX ** -0.5)
    return jnp.einsum('bths,bth->bts', jax.nn.relu(dots), wi.astype(jnp.float32) * (H_IDX ** -0.5))


def _select_keys(scores, q_pos, topk):
    k_pos = jnp.arange(scores.shape[-1], dtype=jnp.int32)
    scores = jnp.where(k_pos[None, None, :] <= q_pos[None, :, None], scores, -jnp.inf)
    _, idx = lax.top_k(scores, topk)
    valid = idx <= q_pos[None, :, None]
    return idx, valid


def _sparse_attend(q, kg, vg, valid):
    s = jnp.einsum('bthd,btkhd->bthk', q, kg, preferred_element_type=jnp.float32) * (DH_B ** -0.5)
    s = jnp.where(valid[:, :, None, :], s, -jnp.inf)
    p = jax.nn.softmax(s, axis=-1).astype(vg.dtype)
    return jnp.einsum('bthk,btkhd->bthd', p, vg)


def _over_query_blocks(fn, arrays, q_pos):
    b, t = arrays[0].shape[:2]
    nb = t // Q_BLOCK
    to_blk = lambda a: jnp.moveaxis(a.reshape((b, nb, Q_BLOCK) + a.shape[2:]), 1, 0)
    out = lax.map(lambda args: fn(*args[0], args[1]),
                  (tuple(to_blk(a) for a in arrays), q_pos.reshape(nb, Q_BLOCK)))
    return jnp.moveaxis(out, 0, 1).reshape((b, t) + out.shape[3:])


def _pool_mixer(xc, buf, pos, w_pool, scale):
    b, t, _ = xc.shape
    ext = jnp.concatenate([buf, xc], 1)
    cs = jnp.cumsum(ext.astype(jnp.float32), axis=1)
    cs0 = jnp.concatenate([jnp.zeros((b, 1, C_WIDTH), jnp.float32), cs], 1)
    end = cs0[:, POOL_BUF + 1:]
    xf = xc.astype(jnp.float32)
    outs = []
    for g, w in enumerate(POOL_WINDOWS):
        sl = slice(g * POOL_CH, (g + 1) * POOL_CH)
        start = cs0[:, POOL_BUF + 1 - w: POOL_BUF + 1 - w + t, sl]
        cnt = jnp.minimum(w, pos + 1).astype(jnp.float32)[None, :, None]
        outs.append((end[..., sl] - start) / cnt - xf[..., sl])
    d = jnp.stack(outs, 2).astype(xc.dtype)
    y = jnp.einsum('btgc,gce->btge', d, w_pool).reshape(b, t, C_WIDTH) * scale
    return y, ext[:, -POOL_BUF:]


def _merge(o_lat, ob, oc, p):
    b, t = ob.shape[:2]
    oa = jnp.einsum('bthr,rhv->bthv', o_lat, p['a_w_uv']).reshape(b, t, A_WIDTH)
    cat = jnp.concatenate([oa, ob.reshape(b, t, B_WIDTH), oc], -1)
    return jnp.einsum('btm,md->btd', cat, p['w_out'])


def _mixers_prompt(x, p):
    b, t, _ = x.shape
    pos = jnp.arange(t, dtype=jnp.int32)
    q_lat, q_rope, ckv_n, kr_r, qb, kb, vb, qi, ki, wi, xc = _project(x, pos, p)
    o_lat = _over_query_blocks(lambda ql, qr, qp: _mla_attend(ql, qr, ckv_n, kr_r, qp, pos), (q_lat, q_rope), pos)
    topk = min(TOPK_MAX, t // 4)
    bi = jnp.arange(b)[:, None, None]

    def dsa_block(q_, qi_, wi_, qp):
        idx, valid = _select_keys(_indexer_scores(qi_, wi_, ki), qp, topk)
        return _sparse_attend(q_, kb[bi, idx], vb[bi, idx], valid)

    ob = _over_query_blocks(dsa_block, (qb, qi, wi), pos)
    oc, pool_state = _pool_mixer(xc, jnp.zeros((b, POOL_BUF, C_WIDTH), xc.dtype), pos, p['c_w_pool'], p['c_scale'])
    return _merge(o_lat, ob, oc, p), (ckv_n, kr_r, kb, vb, ki, pool_state)


def _gather_rows(pool, new, page_table, idx):
    past = page_table.shape[1] * PAGE_SIZE
    sp = jnp.minimum(idx, past - 1)
    bi = jnp.arange(idx.shape[0])[:, None, None]
    from_past = pool[page_table[bi, sp // PAGE_SIZE], sp % PAGE_SIZE]
    from_new = new[bi, jnp.clip(idx - past, 0, new.shape[1] - 1)]
    is_new = (idx >= past).reshape(idx.shape + (1,) * (from_past.ndim - idx.ndim))
    return jnp.where(is_new, from_new, from_past)


def _mixers_sample(x, p, c_ckv, c_krope, c_k, c_v, c_kidx, s_pool, page_table):
    b, t, _ = x.shape
    past = page_table.shape[1] * PAGE_SIZE
    pos = past + jnp.arange(t, dtype=jnp.int32)
    k_pos = jnp.arange(past + t, dtype=jnp.int32)
    q_lat, q_rope, ckv_n, kr_r, qb, kb, vb, qi, ki, wi, xc = _project(x, pos, p)
    ckv_all = jnp.concatenate([c_ckv[page_table].reshape(b, past, KV_RANK), ckv_n], 1)
    kr_all = jnp.concatenate([c_krope[page_table].reshape(b, past, ROPE_A), kr_r], 1)
    o_lat = _mla_attend(q_lat, q_rope, ckv_all, kr_all, pos, k_pos)
    ki_all = jnp.concatenate([c_kidx[page_table].reshape(b, past, D_IDX), ki], 1)
    topk = min(TOPK_MAX, (past + t) // 4)
    idx, valid = _select_keys(_indexer_scores(qi, wi, ki_all), pos, topk)
    ob = _sparse_attend(qb, _gather_rows(c_k, kb, page_table, idx), _gather_rows(c_v, vb, page_table, idx), valid)
    oc, pool_state = _pool_mixer(xc, s_pool, pos, p['c_w_pool'], p['c_scale'])
    return _merge(o_lat, ob, oc, p), (ckv_n, kr_r, kb, vb, ki, pool_state)


def _swiglu(x, wg, wu, wd):
    h = jax.nn.silu(jnp.einsum('btd,df->btf', x, wg)) * jnp.einsum('btd,df->btf', x, wu)
    return jnp.einsum('btf,fd->btd', h, wd)


def _moe(x, router, wg, wu, wd):
    logits = jnp.einsum('btd,de->bte', x, router, preferred_element_type=jnp.float32)
    top_v, top_i = lax.top_k(logits, TOP_K_EXPERTS)
    gates = jax.nn.softmax(top_v, axis=-1)
    dense_gate = jnp.sum(jax.nn.one_hot(top_i, N_EXPERTS, dtype=jnp.float32) * gates[..., None], -2).astype(x.dtype)
    y = jnp.zeros_like(x)
    for e in range(N_EXPERTS):
        y = y + dense_gate[..., e:e + 1] * _swiglu(x, wg[e], wu[e], wd[e])
    return y


def setup_inputs(seed: int = 0) -> dict:
    key = jax.random.key(seed)
    ks = iter(jax.random.split(key, 40))
    nrm = lambda shape, scale: jax.random.normal(next(ks), shape, jnp.float32) * scale
    n_pages = PAST_LEN // PAGE_SIZE
    n_used = DEC_BATCH * n_pages
    n_pool = n_used + (n_used + 3) // 4
    page_table = jax.random.permutation(next(ks), n_pool)[:n_used].reshape(DEC_BATCH, n_pages).astype(jnp.int32)
    return {
        "x_prompt": nrm((BATCH, SEQ, D_MODEL), 1.0),
        "x_sample": nrm((DEC_BATCH, DEC_SEQ, D_MODEL), 1.0),
        "cache_a_ckv": nrm((DEPTH, n_pool, PAGE_SIZE, KV_RANK), 1.0),
        "cache_a_krope": nrm((DEPTH, n_pool, PAGE_SIZE, ROPE_A), 1.0),
        "cache_b_k": nrm((DEPTH, n_pool, PAGE_SIZE, H_B, DH_B), 1.0),
        "cache_b_v": nrm((DEPTH, n_pool, PAGE_SIZE, H_B, DH_B), 1.0),
        "cache_b_kidx": nrm((DEPTH, n_pool, PAGE_SIZE, D_IDX), 1.0),
        "state_pool": nrm((DEPTH, DEC_BATCH, POOL_BUF, C_WIDTH), 1.0),
        "page_table": page_table,
        "w_in": nrm((DEPTH, D_MODEL, N_IN), D_MODEL ** -0.5),
        "a_q_norm": 1.0 + nrm((DEPTH, Q_RANK), 0.02),
        "a_kv_norm": 1.0 + nrm((DEPTH, KV_RANK), 0.02),
        "a_w_uq": nrm((DEPTH, Q_RANK, H_A * (NOPE_A + ROPE_A)), Q_RANK ** -0.5),
        "a_w_uk": nrm((DEPTH, KV_RANK, H_A, NOPE_A), KV_RANK ** -0.5),
        "a_w_uv": nrm((DEPTH, KV_RANK, H_A, V_A), KV_RANK ** -0.5),
        "c_w_pool": nrm((DEPTH, N_POOL_GROUPS, POOL_CH, POOL_CH), POOL_CH ** -0.5),
        "c_scale": 1.0 + nrm((DEPTH, C_WIDTH), 0.02),
        "w_out": nrm((DEPTH, MIX_WIDTH, D_MODEL), BETA * MIX_WIDTH ** -0.5),
        "ln1_g": 1.0 + nrm((DEPTH, D_MODEL), 0.02),
        "ln1_b": nrm((DEPTH, D_MODEL), 0.02),
        "ln2_g": 1.0 + nrm((DEPTH, D_MODEL), 0.02),
        "ln2_b": nrm((DEPTH, D_MODEL), 0.02),
        "ffn_w_gate": nrm((N_DENSE, D_MODEL, D_FF), D_MODEL ** -0.5),
        "ffn_w_up": nrm((N_DENSE, D_MODEL, D_FF), D_MODEL ** -0.5),
        "ffn_w_down": nrm((N_DENSE, D_FF, D_MODEL), BETA * D_FF ** -0.5),
        "moe_router": nrm((N_MOE, D_MODEL, N_EXPERTS), D_MODEL ** -0.5),
        "moe_w_gate": nrm((N_MOE, N_EXPERTS, D_MODEL, D_FF), D_MODEL ** -0.5),
        "moe_w_up": nrm((N_MOE, N_EXPERTS, D_MODEL, D_FF), D_MODEL ** -0.5),
        "moe_w_down": nrm((N_MOE, N_EXPERTS, D_FF, D_MODEL), BETA * D_FF ** -0.5),
    }


def reference(x_prompt, x_sample, cache_a_ckv, cache_a_krope, cache_b_k, cache_b_v, cache_b_kidx, state_pool,
              page_table, w_in, a_q_norm, a_kv_norm, a_w_uq, a_w_uk, a_w_uv, c_w_pool, c_scale, w_out,
              ln1_g, ln1_b, ln2_g, ln2_b, ffn_w_gate, ffn_w_up, ffn_w_down,
              moe_router, moe_w_gate, moe_w_up, moe_w_down):
    hp, hs = x_prompt, x_sample
    new_p, new_s = [], []
    for l in range(DEPTH):
        p = dict(w_in=w_in[l], a_q_norm=a_q_norm[l], a_kv_norm=a_kv_norm[l], a_w_uq=a_w_uq[l],
                 a_w_uk=a_w_uk[l], a_w_uv=a_w_uv[l], c_w_pool=c_w_pool[l], c_scale=c_scale[l], w_out=w_out[l])
        mp, st_p = _mixers_prompt(hp, p)
        ms, st_s = _mixers_sample(hs, p, cache_a_ckv[l], cache_a_krope[l], cache_b_k[l], cache_b_v[l],
                                  cache_b_kidx[l], state_pool[l], page_table)
        hp = layer_norm(ALPHA * hp + mp, ln1_g[l], ln1_b[l])
        hs = layer_norm(ALPHA * hs + ms, ln1_g[l], ln1_b[l])
        j = l // 2
        if l % 2 == 0:
            fp = _swiglu(hp, ffn_w_gate[j], ffn_w_up[j], ffn_w_down[j])
            fs = _swiglu(hs, ffn_w_gate[j], ffn_w_up[j], ffn_w_down[j])
        else:
            fp = _moe(hp, moe_router[j], moe_w_gate[j], moe_w_up[j], moe_w_down[j])
            fs = _moe(hs, moe_router[j], moe_w_gate[j], moe_w_up[j], moe_w_down[j])
        hp = layer_norm(ALPHA * hp + fp, ln2_g[l], ln2_b[l])
        hs = layer_norm(ALPHA * hs + fs, ln2_g[l], ln2_b[l])
        new_p.append(st_p)
        new_s.append(st_s)
    p_ckv, p_krope, p_k, p_v, p_kidx, p_pool = [jnp.stack(a) for a in zip(*new_p)]
    s_ckv, s_krope, s_k, s_v, s_kidx, s_pool = [jnp.stack(a) for a in zip(*new_s)]
    return (hp, hs, p_ckv, p_krope, p_k, p_v, p_kidx, p_pool, s_ckv, s_krope, s_k, s_v, s_kidx, s_pool)
```

```python
import functools

import numpy as np
import jax
import jax.numpy as jnp
from jax import lax
from jax.experimental import pallas as pl
from jax.experimental.pallas import tpu as pltpu

F32 = jnp.float32
BF16 = jnp.bfloat16
I32 = jnp.int32

D_MODEL = 1024
DEPTH = 2
PAGE_SIZE = 128
H_A, NOPE_A, ROPE_A, V_A = 8, 64, 32, 64
Q_RANK, KV_RANK = 256, 256
MLA_SCALE = (NOPE_A + ROPE_A) ** -0.5
H_B, DH_B = 4, 64
H_IDX, D_IDX = 4, 64
TOPK_MAX = 256
POOL_WINDOWS = (2, 4, 8, 16)
POOL_CH = 64
C_WIDTH = len(POOL_WINDOWS) * POOL_CH
POOL_BUF = max(POOL_WINDOWS) - 1
A_WIDTH = H_A * V_A
B_WIDTH = H_B * DH_B
IN_SPLITS = (Q_RANK, KV_RANK, ROPE_A, B_WIDTH, B_WIDTH, B_WIDTH, H_IDX * D_IDX, D_IDX, H_IDX, C_WIDTH)
D_FF = 3584
N_EXPERTS = 8
ROPE_THETA = 10000.0
ALPHA = (2 * DEPTH) ** 0.25
LN_EPS = 1e-5
RMS_EPS = 1e-6

LANES = 128
HALO = 16
QC_W = KV_RANK + LANES
ROPE_LANE0 = 64
AUX_WI0 = 96
INT_MIN = -(2 ** 31)
VMEM_LIMIT = 56 * 1024 * 1024


def _cparams(sem):
    return pltpu.CompilerParams(dimension_semantics=sem, vmem_limit_bytes=VMEM_LIMIT)


def _rms(x, g):
    return x * lax.rsqrt(jnp.mean(x * x, -1, keepdims=True) + RMS_EPS) * g


def _layer_norm(x, g, b):
    mu = jnp.mean(x, -1, keepdims=True)
    xc = x - mu
    var = jnp.mean(xc * xc, -1, keepdims=True)
    return xc * lax.rsqrt(var + LN_EPS) * g + b


def _dot(a, b):
    return jnp.dot(a, b, preferred_element_type=F32)


def _dot_nt(a, b):
    return lax.dot_general(a, b, (((1,), (1,)), ((), ())), preferred_element_type=F32)


def _count(mask):
    return jnp.sum(jnp.where(mask, 1.0, 0.0), axis=-1, keepdims=True)


def _topk_mask(score, valid, kpos, k, idx_bits):
    score = jnp.where(score == 0.0, 0.0, score)
    bits = lax.bitcast_convert_type(score, I32)
    skey = bits ^ ((bits >> 31) & jnp.int32(0x7FFFFFFF))
    skey = jnp.where(valid, skey, jnp.int32(INT_MIN))
    rows = score.shape[0]
    kf = jnp.float32(k)

    def value_step(i, t):
        cand = t + (jnp.int32(1) << (jnp.int32(31) - i))
        return jnp.where(_count(skey >= cand) >= kf, cand, t)

    t = lax.fori_loop(0, 32, value_step, jnp.full((rows, 1), INT_MIN, I32))
    gt = skey > t
    eq = (skey == t) & valid
    need = kf - _count(gt)

    def index_step(i, c):
        cand = c + (jnp.int32(1) << (jnp.int32(idx_bits - 1) - i))
        return jnp.where(_count(eq & (kpos < cand)) < need, cand, c)

    c = lax.fori_loop(0, idx_bits, index_step, jnp.zeros((rows, 1), I32))
    return valid & (gt | (eq & (kpos <= c)))


_G_CQ, _G_CKV, _G_QB, _G_QBR, _G_KB, _G_KBR, _G_VB, _G_QI, _G_QIR, _G_XC = [256 * i for i in range(10)]
_G_A = 2560
_G_AR = 2688
W1_COLS = 2816


def _proj_body(x_ref, w1_ref, wq_ref, wuk_ref, gq_ref, gkv_ref, c64_ref, s64_ref, ca_ref, sa_ref, cq_ref, sq_ref,
               qc_ref, kc_ref, ckv_ref, kb_ref, vb_ref, aux_ref, kbb_ref, vbb_ref, kib_ref, qbb_ref, qih_ref,
               xc_ref):
    x = x_ref[0].astype(BF16)

    def proj(lo, width):
        return _dot(x, w1_ref[:, lo:lo + width])

    c64 = c64_ref[...]
    s64 = s64_ref[...]
    c256 = jnp.concatenate([c64, c64], axis=-1)
    s256 = jnp.concatenate([s64, s64], axis=-1)

    ckv_n = _rms(proj(_G_CKV, 256), gkv_ref[...])
    ckv_ref[0] = ckv_n

    qb = proj(_G_QB, 256) * c256 + proj(_G_QBR, 256) * s256
    qbb_ref[0] = qb.astype(BF16)
    kb = proj(_G_KB, 256) * c256 + proj(_G_KBR, 256) * s256
    kb_ref[0] = kb
    kbb_ref[0] = kb.astype(BF16)
    vb = proj(_G_VB, 256)
    vb_ref[0] = vb
    vbb_ref[0] = vb.astype(BF16)
    qi = proj(_G_QI, 256) * c256 + proj(_G_QIR, 256) * s256
    for h in range(H_IDX):
        qih_ref[0, h] = qi[:, h * D_IDX:(h + 1) * D_IDX].astype(BF16)
    xc_ref[0] = proj(_G_XC, 256)

    aux = proj(_G_A, LANES) * ca_ref[...] + proj(_G_AR, LANES) * sa_ref[...]
    aux_ref[0] = aux
    kib_ref[0] = aux[:, :D_IDX].astype(BF16)
    lane = lax.broadcasted_iota(I32, aux.shape, 1)
    kr_pad = jnp.where((lane >= ROPE_LANE0) & (lane < ROPE_LANE0 + ROPE_A), aux, 0.0)
    kc_ref[0] = jnp.concatenate([ckv_n, kr_pad], axis=-1).astype(BF16)

    cqn = _rms(proj(_G_CQ, 256), gq_ref[...]).astype(BF16)
    cq_t = cq_ref[...]
    sq_t = sq_ref[...]
    for h in range(H_A):
        q_nope = _dot(cqn, wq_ref[:, h * LANES:(h + 1) * LANES]).astype(BF16)
        q_lat = _dot(q_nope, wuk_ref[h])
        q_rope = (_dot(cqn, wq_ref[:, (H_A + h) * LANES:(H_A + h + 1) * LANES]) * cq_t
                  + _dot(cqn, wq_ref[:, (2 * H_A + h) * LANES:(2 * H_A + h + 1) * LANES]) * sq_t)
        qc_ref[0, h] = jnp.concatenate([q_lat, q_rope], axis=-1).astype(BF16)


def _proj_call(x, wts, tabs, tm):
    b, t, d = x.shape
    grid = (b, t // tm)
    tok = lambda w, dt: jax.ShapeDtypeStruct((b, t, w), dt)
    out_shape = (
        jax.ShapeDtypeStruct((b, H_A, t, QC_W), BF16),
        tok(QC_W, BF16),
        tok(KV_RANK, F32),
        tok(B_WIDTH, F32), tok(B_WIDTH, F32),
        tok(LANES, F32),
        tok(B_WIDTH, BF16), tok(B_WIDTH, BF16),
        tok(D_IDX, BF16),
        tok(B_WIDTH, BF16),
        jax.ShapeDtypeStruct((b, H_IDX, t, D_IDX), BF16),
        tok(C_WIDTH, F32),
    )
    tokspec = lambda w: pl.BlockSpec((1, tm, w), lambda bi, i: (bi, i, 0))
    headspec = lambda hh, w: pl.BlockSpec((1, hh, tm, w), lambda bi, i: (bi, 0, i, 0))
    full2 = lambda a: pl.BlockSpec(a.shape, lambda bi, i: (0, 0))
    full3 = lambda a: pl.BlockSpec(a.shape, lambda bi, i: (0, 0, 0))
    tabspec = pl.BlockSpec((tm, LANES), lambda bi, i: (i, 0))
    in_specs = [tokspec(d), full2(wts['w1']), full2(wts['wq']), full3(wts['wuk']), full2(wts['gq']), full2(wts['gkv'])]
    in_specs += [tabspec] * 6
    out_specs = (headspec(H_A, QC_W), tokspec(QC_W), tokspec(KV_RANK), tokspec(B_WIDTH), tokspec(B_WIDTH),
                 tokspec(LANES), tokspec(B_WIDTH), tokspec(B_WIDTH), tokspec(D_IDX), tokspec(B_WIDTH),
                 headspec(H_IDX, D_IDX), tokspec(C_WIDTH))
    return pl.pallas_call(
        _proj_body, out_shape=out_shape, grid=grid, in_specs=in_specs, out_specs=out_specs,
        compiler_params=_cparams(("parallel", "parallel")), name="proj",
    )(x, wts['w1'], wts['wq'], wts['wuk'], wts['gq'], wts['gkv'], *tabs)


def _mla_body(qc_ref, kc_ref, o_ref, m_ref, l_ref, acc_ref, *, tq):
    qi = pl.program_id(1)
    q = qc_ref[0].reshape(H_A * tq, QC_W)
    m_ref[...] = jnp.full(m_ref.shape, -jnp.inf, F32)
    l_ref[...] = jnp.zeros(l_ref.shape, F32)
    acc_ref[...] = jnp.zeros(acc_ref.shape, F32)

    def step(j, masked):
        k = kc_ref[0, pl.ds(pl.multiple_of(j * tq, tq), tq), :]
        s = _dot_nt(q, k) * MLA_SCALE
        if masked:
            row = lax.broadcasted_iota(I32, s.shape, 0)
            col = lax.broadcasted_iota(I32, s.shape, 1)
            s = jnp.where(col <= (row & (tq - 1)), s, -jnp.inf)
        m_prev = m_ref[...]
        m_new = jnp.maximum(m_prev, jnp.max(s, axis=-1, keepdims=True))
        alpha = jnp.exp(m_prev - m_new)
        p = jnp.exp(s - m_new)
        l_ref[...] = alpha * l_ref[...] + jnp.sum(p, axis=-1, keepdims=True)
        acc_ref[...] = alpha * acc_ref[...] + _dot(p.astype(BF16), k[:, :KV_RANK])
        m_ref[...] = m_new

    def full_step(j, carry):
        step(j, False)
        return carry

    lax.fori_loop(0, qi, full_step, 0)
    step(qi, True)
    o = acc_ref[...] / l_ref[...]
    o_ref[0] = o.reshape(H_A, tq, KV_RANK).astype(BF16)


def _mla_call(qc, kc, tq):
    b, _, t, _ = qc.shape
    return pl.pallas_call(
        functools.partial(_mla_body, tq=tq),
        out_shape=jax.ShapeDtypeStruct((b, H_A, t, KV_RANK), BF16),
        grid=(b, t // tq),
        in_specs=[pl.BlockSpec((1, H_A, tq, QC_W), lambda bi, i: (bi, 0, i, 0)),
                  pl.BlockSpec((1, t, QC_W), lambda bi, i: (bi, 0, 0))],
        out_specs=pl.BlockSpec((1, H_A, tq, KV_RANK), lambda bi, i: (bi, 0, i, 0)),
        scratch_shapes=[pltpu.VMEM((H_A * tq, 1), F32), pltpu.VMEM((H_A * tq, 1), F32),
                        pltpu.VMEM((H_A * tq, KV_RANK), F32)],
        compiler_params=_cparams(("parallel", "parallel")), name="mla_prompt",
    )(qc, kc)


def _sparse_heads(q, k, v, sel):
    lane = lax.broadcasted_iota(I32, q.shape, 1)
    out = jnp.zeros(q.shape, F32)
    qf = q.astype(F32)
    for h in range(H_B):
        in_head = (lane >= h * DH_B) & (lane < (h + 1) * DH_B)
        s = _dot_nt(jnp.where(in_head, qf, 0.0).astype(BF16), k) * (DH_B ** -0.5)
        s = jnp.where(sel, s, -jnp.inf)
        p = jnp.exp(s - jnp.max(s, axis=-1, keepdims=True))
        o = _dot(p.astype(BF16), v) / jnp.sum(p, axis=-1, keepdims=True)
        out = jnp.where(in_head, o, out)
    return out


def _dsa_body(qih_ref, aux_ref, kib_ref, qbb_ref, kbb_ref, vbb_ref, ob_ref, *, tq, topk, idx_bits):
    qi = pl.program_id(1)
    dots = _dot_nt(qih_ref[0].reshape(H_IDX * tq, D_IDX), kib_ref[0]) * (D_IDX ** -0.5)
    w = aux_ref[0][:, AUX_WI0:AUX_WI0 + H_IDX] * (H_IDX ** -0.5)
    score = jnp.zeros((tq, dots.shape[1]), F32)
    for h in range(H_IDX):
        score = score + jnp.maximum(dots[h * tq:(h + 1) * tq], 0.0) * w[:, h:h + 1]
    kpos = lax.broadcasted_iota(I32, score.shape, 1)
    qpos = qi * tq + lax.broadcasted_iota(I32, score.shape, 0)
    sel = _topk_mask(score, kpos <= qpos, kpos, topk, idx_bits)
    ob_ref[0] = _sparse_heads(qbb_ref[0], kbb_ref[0], vbb_ref[0], sel).astype(BF16)


def _dsa_call(qih, aux, kib, qbb, kbb, vbb, tq, topk):
    b, t, _ = qbb.shape
    idx_bits = max(1, int(np.ceil(np.log2(t))))
    whole = lambda w: pl.BlockSpec((1, t, w), lambda bi, i: (bi, 0, 0))
    tile = lambda w: pl.BlockSpec((1, tq, w), lambda bi, i: (bi, i, 0))
    return pl.pallas_call(
        functools.partial(_dsa_body, tq=tq, topk=topk, idx_bits=idx_bits),
        out_shape=jax.ShapeDtypeStruct((b, t, B_WIDTH), BF16),
        grid=(b, t // tq),
        in_specs=[pl.BlockSpec((1, H_IDX, tq, D_IDX), lambda bi, i: (bi, 0, i, 0)), tile(LANES), whole(D_IDX),
                  tile(B_WIDTH), whole(B_WIDTH), whole(B_WIDTH)],
        out_specs=tile(B_WIDTH),
        compiler_params=_cparams(("parallel", "parallel")), name="dsa_prompt",
    )(qih, aux, kib, qbb, kbb, vbb)


def _pool_delta(win_sums, cnts, xcur):
    lane = lax.broadcasted_iota(I32, xcur.shape, 1)
    mean = jnp.zeros(xcur.shape, F32)
    for g, w in enumerate(POOL_WINDOWS):
        in_group = (lane >= g * POOL_CH) & (lane < (g + 1) * POOL_CH)
        mean = jnp.where(in_group, win_sums[w] / cnts[w], mean)
    return mean - xcur


def _merge_tail(o_lat_heads, ob, delta, x, wuv_ref, woa_ref, wob_ref, woc_ref, wp_ref, cs_ref, g_ref, b_ref):
    oc = _dot(delta.astype(BF16), wp_ref[...]) * cs_ref[...]
    acc = _dot(ob, wob_ref[...]) + _dot(oc.astype(BF16), woc_ref[...])
    for h in range(H_A):
        oa = _dot(o_lat_heads(h), wuv_ref[h]).astype(BF16)
        acc = acc + _dot(oa, woa_ref[h])
    return _layer_norm(ALPHA * x + acc, g_ref[...], b_ref[...])


def _merge_body(ol_ref, ob_ref, xc_ref, buf_ref, x_ref, wuv_ref, woa_ref, wob_ref, woc_ref, wp_ref, cs_ref, g_ref,
                b_ref, h_ref, ext_ref, *, tm):
    i = pl.program_id(1)
    start = pl.multiple_of(i * tm, tm)

    @pl.when(i == 0)
    def _():
        ext_ref[0:HALO, :] = buf_ref[0]

    @pl.when(i > 0)
    def _():
        ext_ref[0:HALO, :] = xc_ref[0, pl.ds(start - HALO, HALO), :]

    xcur = xc_ref[0, pl.ds(start, tm), :]
    ext_ref[HALO:HALO + tm, :] = xcur
    run = xcur
    win_sums = {}
    for k in range(1, max(POOL_WINDOWS)):
        run = run + ext_ref[HALO - k:HALO - k + tm, :]
        if k + 1 in POOL_WINDOWS:
            win_sums[k + 1] = run
    pos1 = (start + 1 + lax.broadcasted_iota(I32, (tm, 1), 0)).astype(F32)
    cnts = {w: jnp.minimum(jnp.float32(w), pos1) for w in POOL_WINDOWS}
    delta = _pool_delta(win_sums, cnts, xcur)
    h_ref[0] = _merge_tail(lambda h: ol_ref[0, h], ob_ref[0], delta, x_ref[0], wuv_ref, woa_ref, wob_ref, woc_ref,
                           wp_ref, cs_ref, g_ref, b_ref)


def _merge_call(o_lat, ob, xc, buf, x, mw, tm):
    b, t, d = x.shape
    tile = lambda w: pl.BlockSpec((1, tm, w), lambda bi, i: (bi, i, 0))
    full = lambda a: pl.BlockSpec(a.shape, lambda bi, i: (0,) * a.ndim)
    wnames = ('wuv', 'woa', 'wob', 'woc', 'wp', 'cs', 'g1', 'b1')
    return pl.pallas_call(
        functools.partial(_merge_body, tm=tm),
        out_shape=jax.ShapeDtypeStruct((b, t, d), F32),
        grid=(b, t // tm),
        in_specs=[pl.BlockSpec((1, H_A, tm, KV_RANK), lambda bi, i: (bi, 0, i, 0)), tile(B_WIDTH),
                  pl.BlockSpec((1, t, C_WIDTH), lambda bi, i: (bi, 0, 0)),
                  pl.BlockSpec((1, HALO, C_WIDTH), lambda bi, i: (bi, 0, 0)), tile(d)]
                 + [full(mw[n]) for n in wnames],
        out_specs=tile(d),
        scratch_shapes=[pltpu.VMEM((HALO + tm, C_WIDTH), F32)],
        compiler_params=_cparams(("parallel", "arbitrary")), name="merge_prompt",
    )(o_lat, ob, xc, buf, x, *[mw[n] for n in wnames])


def _merge_sample_body(ol_ref, ob_ref, ext_ref, x_ref, wuv_ref, woa_ref, wob_ref, woc_ref, wp_ref, cs_ref, g_ref,
                       b_ref, h_ref, *, n_seen):
    n = ext_ref.shape[1]
    xcur = ext_ref[HALO - 1]
    run = xcur
    win_sums = {}
    for k in range(1, max(POOL_WINDOWS)):
        run = run + ext_ref[HALO - 1 - k]
        if k + 1 in POOL_WINDOWS:
            win_sums[k + 1] = run
    cnts = {w: jnp.full((n, 1), min(w, n_seen), F32) for w in POOL_WINDOWS}
    delta = _pool_delta(win_sums, cnts, xcur)
    h_ref[...] = _merge_tail(lambda h: ol_ref[h], ob_ref[...], delta, x_ref[...], wuv_ref, woa_ref, wob_ref, woc_ref,
                             wp_ref, cs_ref, g_ref, b_ref)


def _merge_sample_call(o_lat_h, ob, ext, x, mw, n_seen):
    n, d = x.shape
    wnames = ('wuv', 'woa', 'wob', 'woc', 'wp', 'cs', 'g1', 'b1')
    return pl.pallas_call(
        functools.partial(_merge_sample_body, n_seen=n_seen),
        out_shape=jax.ShapeDtypeStruct((n, d), F32), name="merge_sample",
        compiler_params=pltpu.CompilerParams(vmem_limit_bytes=VMEM_LIMIT),
    )(o_lat_h, ob, ext, x, *[mw[n_] for n_ in wnames])


def _swiglu_partial(xb, wg, wu, wd):
    a = _dot(xb, wg)
    u = _dot(xb, wu)
    hm = (a / (1.0 + jnp.exp(-a))) * u
    return _dot(hm.astype(BF16), wd)


def _ffn_body(h_ref, wg_ref, wu_ref, wd_ref, g_ref, b_ref, o_ref, xb_ref, acc_ref):
    k = pl.program_id(1)

    @pl.when(k == 0)
    def _():
        xb_ref[...] = h_ref[...].astype(BF16)
        acc_ref[...] = jnp.zeros(acc_ref.shape, F32)

    acc_ref[...] += _swiglu_partial(xb_ref[...], wg_ref[...], wu_ref[...], wd_ref[...])

    @pl.when(k == pl.num_programs(1) - 1)
    def _():
        o_ref[...] = _layer_norm(ALPHA * h_ref[...] + acc_ref[...], g_ref[...], b_ref[...])


def _ffn_call(h, wg, wu, wd, g, bta, tm, tf):
    n, d = h.shape
    f = wg.shape[1]
    return pl.pallas_call(
        _ffn_body, out_shape=jax.ShapeDtypeStruct((n, d), F32), grid=(n // tm, f // tf),
        in_specs=[pl.BlockSpec((tm, d), lambda i, k: (i, 0)), pl.BlockSpec((d, tf), lambda i, k: (0, k)),
                  pl.BlockSpec((d, tf), lambda i, k: (0, k)), pl.BlockSpec((tf, d), lambda i, k: (k, 0)),
                  pl.BlockSpec((1, d), lambda i, k: (0, 0)), pl.BlockSpec((1, d), lambda i, k: (0, 0))],
        out_specs=pl.BlockSpec((tm, d), lambda i, k: (i, 0)),
        scratch_shapes=[pltpu.VMEM((tm, d), BF16), pltpu.VMEM((tm, d), F32)],
        compiler_params=_cparams(("parallel", "arbitrary")), name="ffn_dense",
    )(h, wg, wu, wd, g, bta)


def _router_body(h_ref, r_ref, gate_ref):
    logits = jnp.dot(h_ref[...], r_ref[...], precision=lax.Precision.HIGHEST, preferred_element_type=F32)
    lane = lax.broadcasted_iota(I32, logits.shape, 1).astype(F32)
    logits = jnp.where(lane < N_EXPERTS, logits, -jnp.inf)
    v1 = jnp.max(logits, axis=-1, keepdims=True)
    i1 = jnp.min(jnp.where(logits == v1, lane, float(LANES)), axis=-1, keepdims=True)
    rest = jnp.where(lane == i1, -jnp.inf, logits)
    v2 = jnp.max(rest, axis=-1, keepdims=True)
    i2 = jnp.min(jnp.where(rest == v2, lane, float(LANES)), axis=-1, keepdims=True)
    e2 = jnp.exp(v2 - v1)
    g1 = 1.0 / (1.0 + e2)
    g2 = e2 / (1.0 + e2)
    gate_ref[...] = jnp.where(lane == i1, g1, jnp.where(lane == i2, g2, 0.0))


def _router_call(h, router_pad, tm):
    n, d = h.shape
    return pl.pallas_call(
        _router_body, out_shape=jax.ShapeDtypeStruct((n, LANES), F32), grid=(n // tm,),
        in_specs=[pl.BlockSpec((tm, d), lambda i: (i, 0)), pl.BlockSpec((d, LANES), lambda i: (0, 0))],
        out_specs=pl.BlockSpec((tm, LANES), lambda i: (i, 0)),
        compiler_params=_cparams(("parallel",)), name="moe_router",
    )(h, router_pad)


def _moe_body(h_ref, gate_ref, wg_ref, wu_ref, wd_ref, g_ref, b_ref, o_ref, xb_ref, acc_ref):
    e = pl.program_id(1)
    k = pl.program_id(2)

    @pl.when((e == 0) & (k == 0))
    def _():
        xb_ref[...] = h_ref[...].astype(BF16)
        acc_ref[...] = jnp.zeros(acc_ref.shape, F32)

    gates = gate_ref[...]
    lane = lax.broadcasted_iota(I32, gates.shape, 1)
    ge = jnp.sum(jnp.where(lane == e, gates, 0.0), axis=-1, keepdims=True)
    acc_ref[...] += ge * _swiglu_partial(xb_ref[...], wg_ref[0], wu_ref[0], wd_ref[0])

    @pl.when((e == pl.num_programs(1) - 1) & (k == pl.num_programs(2) - 1))
    def _():
        o_ref[...] = _layer_norm(ALPHA * h_ref[...] + acc_ref[...], g_ref[...], b_ref[...])


def _moe_call(h, gates, wg, wu, wd, g, bta, tm, tf):
    n, d = h.shape
    ne, _, f = wg.shape
    return pl.pallas_call(
        _moe_body, out_shape=jax.ShapeDtypeStruct((n, d), F32), grid=(n // tm, ne, f // tf),
        in_specs=[pl.BlockSpec((tm, d), lambda i, e, k: (i, 0)), pl.BlockSpec((tm, LANES), lambda i, e, k: (i, 0)),
                  pl.BlockSpec((1, d, tf), lambda i, e, k: (e, 0, k)),
                  pl.BlockSpec((1, d, tf), lambda i, e, k: (e, 0, k)),
                  pl.BlockSpec((1, tf, d), lambda i, e, k: (e, k, 0)),
                  pl.BlockSpec((1, d), lambda i, e, k: (0, 0)), pl.BlockSpec((1, d), lambda i, e, k: (0, 0))],
        out_specs=pl.BlockSpec((tm, d), lambda i, e, k: (i, 0)),
        scratch_shapes=[pltpu.VMEM((tm, d), BF16), pltpu.VMEM((tm, d), F32)],
        compiler_params=_cparams(("parallel", "arbitrary", "arbitrary")), name="ffn_moe",
    )(h, gates, wg, wu, wd, g, bta)


def _mla_sample_body(pt_ref, q_ref, knew_ref, *refs, npg):
    ck_refs = refs[:npg]
    kr_refs = refs[npg:2 * npg]
    o_ref, m_ref, l_ref, acc_ref = refs[2 * npg:]
    s_i = pl.program_id(1)

    @pl.when(s_i == 0)
    def _():
        m_ref[...] = jnp.full(m_ref.shape, -jnp.inf, F32)
        l_ref[...] = jnp.zeros(l_ref.shape, F32)
        acc_ref[...] = jnp.zeros(acc_ref.shape, F32)

    q = q_ref[0]
    q_rope = q[:, KV_RANK + ROPE_LANE0:KV_RANK + ROPE_LANE0 + ROPE_A]
    ck = jnp.concatenate([r[0, 0] for r in ck_refs], axis=0).astype(BF16)
    kr = jnp.concatenate([r[0, 0] for r in kr_refs], axis=0).astype(BF16)
    s = (_dot_nt(q[:, :KV_RANK], ck) + _dot_nt(q_rope, kr)) * MLA_SCALE
    m_prev = m_ref[...]
    m_new = jnp.maximum(m_prev, jnp.max(s, axis=-1, keepdims=True))
    alpha = jnp.exp(m_prev - m_new)
    p = jnp.exp(s - m_new)
    l_ref[...] = alpha * l_ref[...] + jnp.sum(p, axis=-1, keepdims=True)
    acc_ref[...] = alpha * acc_ref[...] + _dot(p.astype(BF16), ck)
    m_ref[...] = m_new

    @pl.when(s_i == pl.num_programs(1) - 1)
    def _():
        kn = knew_ref[0].astype(F32)
        s_n = jnp.sum(q.astype(F32) * kn, axis=-1, keepdims=True) * MLA_SCALE
        m_old = m_ref[...]
        m_fin = jnp.maximum(m_old, s_n)
        a = jnp.exp(m_old - m_fin)
        p_n = jnp.exp(s_n - m_fin)
        l_fin = a * l_ref[...] + p_n
        acc = a * acc_ref[...] + p_n * kn[:, :KV_RANK]
        o_ref[0] = (acc / l_fin).astype(BF16)


def _mla_sample_call(page_table, q, knew, cache_ckv, cache_kr, layer, npg):
    n, n_pages = page_table.shape
    steps = n_pages // npg
    ck_specs = [pl.BlockSpec((1, 1, PAGE_SIZE, KV_RANK),
                             functools.partial(lambda bi, s, pt, j: (layer, pt[bi, s * npg + j], 0, 0), j=j))
                for j in range(npg)]
    kr_specs = [pl.BlockSpec((1, 1, PAGE_SIZE, ROPE_A),
                             functools.partial(lambda bi, s, pt, j: (layer, pt[bi, s * npg + j], 0, 0), j=j))
                for j in range(npg)]
    grid_spec = pltpu.PrefetchScalarGridSpec(
        num_scalar_prefetch=1, grid=(n, steps),
        in_specs=[pl.BlockSpec((1, H_A, QC_W), lambda bi, s, pt: (bi, 0, 0)),
                  pl.BlockSpec((1, 1, QC_W), lambda bi, s, pt: (bi, 0, 0))] + ck_specs + kr_specs,
        out_specs=pl.BlockSpec((1, H_A, KV_RANK), lambda bi, s, pt: (bi, 0, 0)),
        scratch_shapes=[pltpu.VMEM((H_A, 1), F32), pltpu.VMEM((H_A, 1), F32), pltpu.VMEM((H_A, KV_RANK), F32)])
    return pl.pallas_call(
        functools.partial(_mla_sample_body, npg=npg),
        out_shape=jax.ShapeDtypeStruct((n, H_A, KV_RANK), BF16), grid_spec=grid_spec,
        compiler_params=_cparams(("parallel", "arbitrary")), name="mla_sample",
    )(page_table, q, knew, *([cache_ckv] * npg), *([cache_kr] * npg))


def _idx_sample_body(pt_ref, q_ref, w_ref, *refs, npg):
    ki_refs = refs[:npg]
    o_ref = refs[npg]
    ki = jnp.concatenate([r[0, 0] for r in ki_refs], axis=0).astype(BF16)
    dots = _dot_nt(q_ref[0], ki) * (D_IDX ** -0.5)
    w = w_ref[0] * (H_IDX ** -0.5)
    o_ref[0] = jnp.sum(jnp.maximum(dots, 0.0) * w, axis=0, keepdims=True)


def _idx_sample_call(page_table, qi, wi, cache_kidx, layer, npg):
    n, n_pages = page_table.shape
    steps = n_pages // npg
    ki_specs = [pl.BlockSpec((1, 1, PAGE_SIZE, D_IDX),
                             functools.partial(lambda bi, s, pt, j: (layer, pt[bi, s * npg + j], 0, 0), j=j))
                for j in range(npg)]
    grid_spec = pltpu.PrefetchScalarGridSpec(
        num_scalar_prefetch=1, grid=(n, steps),
        in_specs=[pl.BlockSpec((1, H_IDX, D_IDX), lambda bi, s, pt: (bi, 0, 0)),
                  pl.BlockSpec((1, H_IDX, 1), lambda bi, s, pt: (bi, 0, 0))] + ki_specs,
        out_specs=pl.BlockSpec((1, 1, npg * PAGE_SIZE), lambda bi, s, pt: (bi, 0, s)))
    return pl.pallas_call(
        functools.partial(_idx_sample_body, npg=npg),
        out_shape=jax.ShapeDtypeStruct((n, 1, n_pages * PAGE_SIZE), F32), grid_spec=grid_spec,
        compiler_params=_cparams(("parallel", "arbitrary")), name="idx_sample",
    )(page_table, qi, wi, *([cache_kidx] * npg))


def _select_sample_body(sp_ref, qi_ref, w_ref, kin_ref, idx_ref, sel_ref, *, past, topk, idx_bits):
    n = sp_ref.shape[0]
    kn = kin_ref[...].astype(F32)
    s_new = jnp.zeros((n, 1), F32)
    for h in range(H_IDX):
        d = jnp.sum(qi_ref[h].astype(F32) * kn, axis=-1, keepdims=True) * (D_IDX ** -0.5)
        s_new = s_new + jnp.maximum(d, 0.0) * (w_ref[:, h:h + 1] * (H_IDX ** -0.5))
    lane = lax.broadcasted_iota(I32, (n, LANES), 1)
    tail = jnp.where(lane == 0, s_new, 0.0)
    score = jnp.concatenate([sp_ref[...], tail], axis=-1)
    kpos = lax.broadcasted_iota(I32, score.shape, 1)
    sel = _topk_mask(score, kpos <= past, kpos, topk, idx_bits)
    kposf = kpos.astype(F32)
    sel_ref[...] = jnp.where(sel, kposf, jnp.float32(2 ** 30))
    slot = lax.broadcasted_iota(I32, (n, topk), 1)

    def extract(j, out):
        cur = jnp.min(sel_ref[...], axis=-1, keepdims=True)
        sel_ref[...] = jnp.where(sel_ref[...] == cur, jnp.float32(2 ** 30), sel_ref[...])
        return jnp.where(slot == j, cur.astype(I32), out)

    idx_ref[...] = lax.fori_loop(0, topk, extract, jnp.zeros((n, topk), I32))


def _select_sample_call(score_past, qi, wi, ki_new, topk):
    n, past = score_past.shape
    idx_bits = int(np.ceil(np.log2(past + LANES)))
    return pl.pallas_call(
        functools.partial(_select_sample_body, past=past, topk=topk, idx_bits=idx_bits),
        out_shape=jax.ShapeDtypeStruct((n, topk), I32),
        scratch_shapes=[pltpu.VMEM((n, past + LANES), F32)],
        compiler_params=pltpu.CompilerParams(vmem_limit_bytes=VMEM_LIMIT), name="select_sample",
    )(score_past, qi, wi, ki_new)


def _dsa_sample_body(pt_ref, idx_ref, q_ref, ck_hbm, cv_hbm, kn_hbm, vn_hbm, o_ref, kbuf, vbuf, sem, *,
                     layer, past, topk):
    b = pl.program_id(0)

    def row_copies(src_k, src_v, j):
        return (pltpu.make_async_copy(src_k, kbuf.at[pl.ds(j, 1), :], sem.at[0]),
                pltpu.make_async_copy(src_v, vbuf.at[pl.ds(j, 1), :], sem.at[1]))

    def past_copies(j, i):
        page = pt_ref[b, i // PAGE_SIZE]
        row = i % PAGE_SIZE
        return row_copies(ck_hbm.at[layer, page, pl.ds(row, 1), :], cv_hbm.at[layer, page, pl.ds(row, 1), :], j)

    def start(j, carry):
        i = idx_ref[b, j]

        @pl.when(i < past)
        def _():
            for c in past_copies(j, i):
                c.start()

        @pl.when(i >= past)
        def _():
            for c in row_copies(kn_hbm.at[pl.ds(b, 1), :], vn_hbm.at[pl.ds(b, 1), :], j):
                c.start()

        return carry

    def wait(j, carry):
        for c in past_copies(j, 0):
            c.wait()
        return carry

    lax.fori_loop(0, topk, start, 0)
    lax.fori_loop(0, topk, wait, 0)
    q = q_ref[0]
    row = lax.broadcasted_iota(I32, (8, B_WIDTH), 0)
    lane = lax.broadcasted_iota(I32, (8, B_WIDTH), 1)
    own = (lane >= row * DH_B) & (lane < (row + 1) * DH_B)
    qh = jnp.where(own, jnp.broadcast_to(q.astype(F32), (8, B_WIDTH)), 0.0).astype(BF16)
    s = _dot_nt(qh, kbuf[...].astype(BF16)) * (DH_B ** -0.5)
    p = jnp.exp(s - jnp.max(s, axis=-1, keepdims=True))
    o = _dot(p.astype(BF16), vbuf[...].astype(BF16)) / jnp.sum(p, axis=-1, keepdims=True)
    o_ref[0] = jnp.sum(jnp.where(own, o, 0.0), axis=0, keepdims=True).astype(BF16)


def _dsa_sample_call(page_table, idx, qb, cache_k, cache_v, kb_new, vb_new, layer):
    n, n_pages = page_table.shape
    topk = idx.shape[1]
    past = n_pages * PAGE_SIZE
    any_spec = pl.BlockSpec(memory_space=pl.ANY)
    grid_spec = pltpu.PrefetchScalarGridSpec(
        num_scalar_prefetch=2, grid=(n,),
        in_specs=[pl.BlockSpec((1, 1, B_WIDTH), lambda bi, pt, ix: (bi, 0, 0)), any_spec, any_spec, any_spec,
                  any_spec],
        out_specs=pl.BlockSpec((1, 1, B_WIDTH), lambda bi, pt, ix: (bi, 0, 0)),
        scratch_shapes=[pltpu.VMEM((topk, B_WIDTH), F32), pltpu.VMEM((topk, B_WIDTH), F32),
                        pltpu.SemaphoreType.DMA((2,))])
    return pl.pallas_call(
        functools.partial(_dsa_sample_body, layer=layer, past=past, topk=topk),
        out_shape=jax.ShapeDtypeStruct((n, 1, B_WIDTH), BF16), grid_spec=grid_spec,
        compiler_params=_cparams(("arbitrary",)), name="dsa_sample",
    )(page_table, idx, qb, cache_k, cache_v, kb_new, vb_new)


def _rot_cols(w, head_dim):
    k, n = w.shape
    wh = w.reshape(k, n // head_dim, 2, head_dim // 2)
    return jnp.concatenate([-wh[:, :, 1], wh[:, :, 0]], axis=-1).reshape(k, n)


def _prep_layer(w_in, a_q_norm, a_kv_norm, a_w_uq, a_w_uk, a_w_uv, c_w_pool, c_scale, w_out, ln_g, ln_b):
    offs = np.cumsum((0,) + IN_SPLITS)
    cq, ckv, kr, qb, kb, vb, qi, ki, wi, xc = [w_in[:, offs[i]:offs[i + 1]] for i in range(len(IN_SPLITS))]
    d = w_in.shape[0]
    zeros = lambda n: jnp.zeros((d, n), F32)
    grp_a = jnp.concatenate([ki, kr, wi, zeros(LANES - D_IDX - ROPE_A - H_IDX)], axis=1)
    grp_ar = jnp.concatenate([_rot_cols(ki, D_IDX), _rot_cols(kr, ROPE_A), zeros(LANES - D_IDX - ROPE_A)], axis=1)
    w1 = jnp.concatenate([cq, ckv, qb, _rot_cols(qb, DH_B), kb, _rot_cols(kb, DH_B), vb, qi, _rot_cols(qi, D_IDX),
                          xc, grp_a, grp_ar], axis=1).astype(BF16)
    uq = a_w_uq.reshape(Q_RANK, H_A, NOPE_A + ROPE_A)
    nope = jnp.pad(uq[:, :, :NOPE_A], ((0, 0), (0, 0), (0, LANES - NOPE_A)))
    rope = uq[:, :, NOPE_A:]
    rope_rot = jnp.concatenate([-rope[..., ROPE_A // 2:], rope[..., :ROPE_A // 2]], axis=-1)
    pad_rope = lambda r: jnp.pad(r, ((0, 0), (0, 0), (ROPE_LANE0, LANES - ROPE_LANE0 - ROPE_A)))
    wq = jnp.concatenate([nope.reshape(Q_RANK, -1), pad_rope(rope).reshape(Q_RANK, -1),
                          pad_rope(rope_rot).reshape(Q_RANK, -1)], axis=1).astype(BF16)
    wuk = jnp.pad(jnp.transpose(a_w_uk, (1, 2, 0)), ((0, 0), (0, LANES - NOPE_A), (0, 0))).astype(BF16)
    proj_w = dict(w1=w1, wq=wq, wuk=wuk, gq=a_q_norm.reshape(1, -1), gkv=a_kv_norm.reshape(1, -1))
    wuv = jnp.pad(jnp.transpose(a_w_uv, (1, 0, 2)), ((0, 0), (0, 0), (0, LANES - V_A))).astype(BF16)
    woa = jnp.pad(w_out[:A_WIDTH].reshape(H_A, V_A, -1), ((0, 0), (0, LANES - V_A), (0, 0))).astype(BF16)
    wp = jnp.zeros((C_WIDTH, C_WIDTH), F32)
    for g in range(len(POOL_WINDOWS)):
        wp = wp.at[g * POOL_CH:(g + 1) * POOL_CH, g * POOL_CH:(g + 1) * POOL_CH].set(c_w_pool[g])
    merge_w = dict(wuv=wuv, woa=woa, wob=w_out[A_WIDTH:A_WIDTH + B_WIDTH].astype(BF16),
                   woc=w_out[A_WIDTH + B_WIDTH:].astype(BF16), wp=wp.astype(BF16), cs=c_scale.reshape(1, -1),
                   g1=ln_g.reshape(1, -1), b1=ln_b.reshape(1, -1))
    return proj_w, merge_w


def _rope_tables(pos):
    posf = pos.astype(F32)[:, None]

    def cs(dim):
        inv = ROPE_THETA ** (-jnp.arange(0, dim, 2, dtype=F32) / dim)
        ang = posf * inv[None, :]
        c, s = jnp.cos(ang), jnp.sin(ang)
        return jnp.concatenate([c, c], -1), jnp.concatenate([s, s], -1)

    c64, s64 = cs(DH_B)
    c32, s32 = cs(ROPE_A)
    t = pos.shape[0]
    z = lambda n: jnp.zeros((t, n), F32)
    c_pair, s_pair = jnp.concatenate([c64, c64], -1), jnp.concatenate([s64, s64], -1)
    c_a = jnp.concatenate([c64, c32, jnp.ones((t, H_IDX), F32), z(LANES - D_IDX - ROPE_A - H_IDX)], -1)
    s_a = jnp.concatenate([s64, s32, z(LANES - D_IDX - ROPE_A)], -1)
    c_q = jnp.concatenate([z(ROPE_LANE0), c32, z(LANES - ROPE_LANE0 - ROPE_A)], -1)
    s_q = jnp.concatenate([z(ROPE_LANE0), s32, z(LANES - ROPE_LANE0 - ROPE_A)], -1)
    return c_pair, s_pair, c_a, s_a, c_q, s_q


def _aux_split(aux):
    return aux[..., D_IDX:D_IDX + ROPE_A], aux[..., :D_IDX]


def _mixers_prompt(x, proj_w, merge_w, tabs, tiles):
    b, t, _ = x.shape
    (qc, kc, ckv_n, kb, vb, aux, kbb, vbb, kib, qbb, qih, xc) = _proj_call(x, proj_w, tabs, tiles['proj'])
    o_lat = _mla_call(qc, kc, tiles['mla'])
    ob = _dsa_call(qih, aux, kib, qbb, kbb, vbb, tiles['dsa'], min(TOPK_MAX, t // 4))
    buf = jnp.zeros((b, HALO, C_WIDTH), F32)
    h = _merge_call(o_lat, ob, xc, buf, x, merge_w, tiles['merge'])
    kr_r, ki_r = _aux_split(aux)
    state = (ckv_n, kr_r, kb.reshape(b, t, H_B, DH_B), vb.reshape(b, t, H_B, DH_B), ki_r, xc[:, t - POOL_BUF:])
    return h, state


def _mixers_sample(x, proj_w, merge_w, tabs, caches, state_pool, page_table, layer, npg):
    n = x.shape[0]
    c_ckv, c_kr, c_k, c_v, c_kidx = caches
    (qc, kc, ckv_n, kb, vb, aux, kbb, vbb, kib, qbb, qih, xc) = _proj_call(x.reshape(1, n, -1), proj_w, tabs, n)
    q = jnp.transpose(qc[0], (1, 0, 2))
    o_lat = _mla_sample_call(page_table, q, kc[0][:, None, :], c_ckv, c_kr, layer, npg)
    qi = jnp.transpose(qih[0], (1, 0, 2))
    wi = aux[0][:, AUX_WI0:AUX_WI0 + H_IDX]
    score_past = _idx_sample_call(page_table, qi, wi[:, :, None], c_kidx, layer, npg)[:, 0, :]
    past = score_past.shape[1]
    idx = _select_sample_call(score_past, qih[0], wi, kib[0], min(TOPK_MAX, (past + 1) // 4))
    ob = _dsa_sample_call(page_table, idx, qbb[0][:, None, :], c_k, c_v, kb[0], vb[0], layer)[:, 0, :]
    ext = jnp.concatenate([state_pool, xc[0][:, None, :]], axis=1)
    h = _merge_sample_call(jnp.transpose(o_lat, (1, 0, 2)), ob, jnp.transpose(ext, (1, 0, 2)), x, merge_w, past + 1)
    kr_r, ki_r = _aux_split(aux[0])
    state = (ckv_n[0][:, None], kr_r[:, None], kb[0].reshape(n, 1, H_B, DH_B), vb[0].reshape(n, 1, H_B, DH_B),
             ki_r[:, None], ext[:, HALO - POOL_BUF:])
    return h, state


def _pick_tile(n, pref):
    t = min(n, pref)
    while n % t:
        t //= 2
    return t


def kernel(x_prompt, x_sample, cache_a_ckv, cache_a_krope, cache_b_k, cache_b_v, cache_b_kidx, state_pool, page_table, w_in, a_q_norm, a_kv_norm, a_w_uq, a_w_uk, a_w_uv, c_w_pool, c_scale, w_out, ln1_g, ln1_b, ln2_g, ln2_b, ffn_w_gate, ffn_w_up, ffn_w_down, moe_router, moe_w_gate, moe_w_up, moe_w_down):
    bsz, seq, d = x_prompt.shape
    n_dec = x_sample.shape[0]
    depth = w_in.shape[0]
    n_pages = page_table.shape[1]
    past = n_pages * PAGE_SIZE
    n_pool = cache_b_k.shape[1]
    cache_k = cache_b_k.reshape(depth, n_pool, PAGE_SIZE, B_WIDTH)
    cache_v = cache_b_v.reshape(depth, n_pool, PAGE_SIZE, B_WIDTH)
    caches = (cache_a_ckv, cache_a_krope, cache_k, cache_v, cache_b_kidx)
    tiles = dict(proj=_pick_tile(seq, 512), mla=_pick_tile(seq, 256), dsa=_pick_tile(seq, 128),
                 merge=_pick_tile(seq, 512))
    npg = _pick_tile(n_pages, 16)
    tabs_p = _rope_tables(jnp.arange(seq, dtype=I32))
    tabs_s = _rope_tables(jnp.full((n_dec,), past, I32))
    hp = x_prompt
    hs = x_sample.reshape(n_dec, d)
    new_p, new_s = [], []
    for l in range(depth):
        proj_w, merge_w = _prep_layer(w_in[l], a_q_norm[l], a_kv_norm[l], a_w_uq[l], a_w_uk[l], a_w_uv[l],
                                      c_w_pool[l], c_scale[l], w_out[l], ln1_g[l], ln1_b[l])
        hp, st_p = _mixers_prompt(hp, proj_w, merge_w, tabs_p, tiles)
        hs, st_s = _mixers_sample(hs, proj_w, merge_w, tabs_s, caches, state_pool[l], page_table, l, npg)
        g2, b2 = ln2_g[l].reshape(1, -1), ln2_b[l].reshape(1, -1)
        hp2 = hp.reshape(bsz * seq, d)
        j = l // 2
        tm = _pick_tile(bsz * seq, 1024)
        if l % 2 == 0:
            wg, wu, wd = ffn_w_gate[j].astype(BF16), ffn_w_up[j].astype(BF16), ffn_w_down[j].astype(BF16)
            hp2 = _ffn_call(hp2, wg, wu, wd, g2, b2, tm, 512)
            hs = _ffn_call(hs, wg, wu, wd, g2, b2, n_dec, 512)
        else:
            wg, wu, wd = moe_w_gate[j].astype(BF16), moe_w_up[j].astype(BF16), moe_w_down[j].astype(BF16)
            router = jnp.pad(moe_router[j], ((0, 0), (0, LANES - N_EXPERTS)))
            hp2 = _moe_call(hp2, _router_call(hp2, router, tm), wg, wu, wd, g2, b2, tm, 512)
            hs = _moe_call(hs, _router_call(hs, router, n_dec), wg, wu, wd, g2, b2, n_dec, 512)
        hp = hp2.reshape(bsz, seq, d)
        new_p.append(st_p)
        new_s.append(st_s)
    outs_p = [jnp.stack(a) for a in zip(*new_p)]
    outs_s = [jnp.stack(a) for a in zip(*new_s)]
    return (hp, hs.reshape(n_dec, 1, d), *outs_p, *outs_s)
```

```python
import functools

import numpy as np
import jax
import jax.numpy as jnp
from jax import lax
from jax.experimental import pallas as pl
from jax.experimental.pallas import tpu as pltpu

F32 = jnp.float32
BF16 = jnp.bfloat16
I32 = jnp.int32

D_MODEL = 1024
DEPTH = 2
PAGE_SIZE = 128
H_A, NOPE_A, ROPE_A, V_A = 8, 64, 32, 64
Q_RANK, KV_RANK = 256, 256
MLA_SCALE = (NOPE_A + ROPE_A) ** -0.5
H_B, DH_B = 4, 64
H_IDX, D_IDX = 4, 64
TOPK_MAX = 256
POOL_WINDOWS = (2, 4, 8, 16)
POOL_CH = 64
C_WIDTH = len(POOL_WINDOWS) * POOL_CH
POOL_BUF = max(POOL_WINDOWS) - 1
A_WIDTH = H_A * V_A
B_WIDTH = H_B * DH_B
IN_SPLITS = (Q_RANK, KV_RANK, ROPE_A, B_WIDTH, B_WIDTH, B_WIDTH, H_IDX * D_IDX, D_IDX, H_IDX, C_WIDTH)
D_FF = 3584
N_EXPERTS = 8
ROPE_THETA = 10000.0
ALPHA = (2 * DEPTH) ** 0.25
LN_EPS = 1e-5
RMS_EPS = 1e-6

LANES = 128
HALO = 16
QC_W = KV_RANK + LANES
ROPE_LANE0 = 64
AUX_WI0 = 96
INT_MIN = -(2 ** 31)
LOG2E = 1.4426950408889634
ROUTE_LANE0 = N_EXPERTS
MOE_ROW_TILE = 512
VMEM_LIMIT = 56 * 1024 * 1024


def _cparams(sem):
    return pltpu.CompilerParams(dimension_semantics=sem, vmem_limit_bytes=VMEM_LIMIT)


def _rms(x, g):
    return x * lax.rsqrt(jnp.mean(x * x, -1, keepdims=True) + RMS_EPS) * g


def _layer_norm(x, g, b):
    mu = jnp.mean(x, -1, keepdims=True)
    xc = x - mu
    var = jnp.mean(xc * xc, -1, keepdims=True)
    return xc * lax.rsqrt(var + LN_EPS) * g + b


def _dot(a, b):
    return jnp.dot(a, b, preferred_element_type=F32)


def _dot_nt(a, b):
    return lax.dot_general(a, b, (((1,), (1,)), ((), ())), preferred_element_type=F32)


def _count(mask):
    return jnp.sum(jnp.where(mask, 1.0, 0.0), axis=-1, keepdims=True)


def _topk_mask(score, valid, kpos, k, idx_bits):
    skey = jnp.where(valid, _order_key(score), jnp.int32(INT_MIN))
    rows = score.shape[0]
    kf = jnp.float32(k)

    def value_step(i, t):
        cand = t + (jnp.int32(1) << (jnp.int32(31) - i))
        return jnp.where(_count(skey >= cand) >= kf, cand, t)

    t = lax.fori_loop(0, 32, value_step, jnp.full((rows, 1), INT_MIN, I32))
    gt = skey > t
    eq = (skey == t) & valid
    need = kf - _count(gt)

    def index_step(i, c):
        cand = c + (jnp.int32(1) << (jnp.int32(idx_bits - 1) - i))
        return jnp.where(_count(eq & (kpos < cand)) < need, cand, c)

    c = lax.fori_loop(0, idx_bits, index_step, jnp.zeros((rows, 1), I32))
    return valid & (gt | (eq & (kpos <= c)))


_G_CQ, _G_CKV, _G_QB, _G_QBR, _G_KB, _G_KBR, _G_VB, _G_QI, _G_QIR, _G_XC = [256 * i for i in range(10)]
_G_A = 2560
_G_AR = 2688
W1_COLS = 2816


def _proj_body(x_ref, w1_ref, wq_ref, wuk_ref, gq_ref, gkv_ref, c64_ref, s64_ref, ca_ref, sa_ref, cq_ref, sq_ref,
               qc_ref, kc_ref, ckv_ref, kb_ref, vb_ref, aux_ref, kbb_ref, vbb_ref, kib_ref, qbb_ref, qih_ref,
               xc_ref):
    x = x_ref[0].astype(BF16)

    def proj(lo, width):
        return _dot(x, w1_ref[:, lo:lo + width])

    c64 = c64_ref[...]
    s64 = s64_ref[...]
    c256 = jnp.concatenate([c64, c64], axis=-1)
    s256 = jnp.concatenate([s64, s64], axis=-1)

    ckv_n = _rms(proj(_G_CKV, 256), gkv_ref[...])
    ckv_ref[0] = ckv_n

    qb = proj(_G_QB, 256) * c256 + proj(_G_QBR, 256) * s256
    qbb_ref[0] = qb.astype(BF16)
    kb = proj(_G_KB, 256) * c256 + proj(_G_KBR, 256) * s256
    kb_ref[0] = kb
    kbb_ref[0] = kb.astype(BF16)
    vb = proj(_G_VB, 256)
    vb_ref[0] = vb
    vbb_ref[0] = vb.astype(BF16)
    qi = proj(_G_QI, 256) * c256 + proj(_G_QIR, 256) * s256
    for h in range(H_IDX):
        qih_ref[0, h] = qi[:, h * D_IDX:(h + 1) * D_IDX].astype(BF16)
    xc_ref[0] = proj(_G_XC, 256)

    aux = proj(_G_A, LANES) * ca_ref[...] + proj(_G_AR, LANES) * sa_ref[...]
    aux_ref[0] = aux
    kib_ref[0] = aux[:, :D_IDX].astype(BF16)
    lane = lax.broadcasted_iota(I32, aux.shape, 1)
    kr_pad = jnp.where((lane >= ROPE_LANE0) & (lane < ROPE_LANE0 + ROPE_A), aux, 0.0)
    kc_ref[0] = jnp.concatenate([ckv_n, kr_pad], axis=-1).astype(BF16)

    cqn = _rms(proj(_G_CQ, 256), gq_ref[...]).astype(BF16)
    cq_t = cq_ref[...]
    sq_t = sq_ref[...]
    for h in range(H_A):
        q_nope = _dot(cqn, wq_ref[:, h * LANES:(h + 1) * LANES]).astype(BF16)
        q_lat = _dot(q_nope, wuk_ref[h])
        q_rope = (_dot(cqn, wq_ref[:, (H_A + h) * LANES:(H_A + h + 1) * LANES]) * cq_t
                  + _dot(cqn, wq_ref[:, (2 * H_A + h) * LANES:(2 * H_A + h + 1) * LANES]) * sq_t)
        qc_ref[0, h] = jnp.concatenate([q_lat, q_rope], axis=-1).astype(BF16)


def _proj_call(x, wts, tabs, tm):
    b, t, d = x.shape
    grid = (b, t // tm)
    tok = lambda w, dt: jax.ShapeDtypeStruct((b, t, w), dt)
    out_shape = (
        jax.ShapeDtypeStruct((b, H_A, t, QC_W), BF16),
        tok(QC_W, BF16),
        tok(KV_RANK, F32),
        tok(B_WIDTH, F32), tok(B_WIDTH, F32),
        tok(LANES, F32),
        tok(B_WIDTH, BF16), tok(B_WIDTH, BF16),
        tok(D_IDX, BF16),
        tok(B_WIDTH, BF16),
        jax.ShapeDtypeStruct((b, H_IDX, t, D_IDX), BF16),
        tok(C_WIDTH, F32),
    )
    tokspec = lambda w: pl.BlockSpec((1, tm, w), lambda bi, i: (bi, i, 0))
    headspec = lambda hh, w: pl.BlockSpec((1, hh, tm, w), lambda bi, i: (bi, 0, i, 0))
    full2 = lambda a: pl.BlockSpec(a.shape, lambda bi, i: (0, 0))
    full3 = lambda a: pl.BlockSpec(a.shape, lambda bi, i: (0, 0, 0))
    tabspec = pl.BlockSpec((tm, LANES), lambda bi, i: (i, 0))
    in_specs = [tokspec(d), full2(wts['w1']), full2(wts['wq']), full3(wts['wuk']), full2(wts['gq']), full2(wts['gkv'])]
    in_specs += [tabspec] * 6
    out_specs = (headspec(H_A, QC_W), tokspec(QC_W), tokspec(KV_RANK), tokspec(B_WIDTH), tokspec(B_WIDTH),
                 tokspec(LANES), tokspec(B_WIDTH), tokspec(B_WIDTH), tokspec(D_IDX), tokspec(B_WIDTH),
                 headspec(H_IDX, D_IDX), tokspec(C_WIDTH))
    return pl.pallas_call(
        _proj_body, out_shape=out_shape, grid=grid, in_specs=in_specs, out_specs=out_specs,
        compiler_params=_cparams(("parallel", "parallel")), name="proj",
    )(x, wts['w1'], wts['wq'], wts['wuk'], wts['gq'], wts['gkv'], *tabs)


def _flash_init(m_ref, l_ref, acc_ref):
    m_ref[...] = jnp.full(m_ref.shape, -jnp.inf, F32)
    l_ref[...] = jnp.zeros(l_ref.shape, F32)
    acc_ref[...] = jnp.zeros(acc_ref.shape, F32)


def _lane_tile(x, width):
    return x if width == LANES else jnp.concatenate([x] * (width // LANES), axis=1)


def _lane_fold(x):
    out = x[:, :LANES]
    for i in range(1, x.shape[1] // LANES):
        out = out + x[:, i * LANES:(i + 1) * LANES]
    return out


def _flash_step(q, k, v, keep, m_ref, l_ref, acc_ref, scale):
    s = _dot_nt(q, k) * (scale * LOG2E)
    if keep is not None:
        s = jnp.where(keep, s, -jnp.inf)
    m_prev = m_ref[...]
    m_new = jnp.maximum(m_prev, jnp.max(s, axis=-1, keepdims=True))
    m_use = jnp.where(m_new == -jnp.inf, 0.0, m_new)
    alpha = jnp.exp2(m_prev - m_use)
    p = jnp.exp2(s - _lane_tile(m_use, s.shape[1]))
    l_ref[...] = alpha * l_ref[...] + _lane_fold(p)
    acc_ref[...] = _lane_tile(alpha, v.shape[1]) * acc_ref[...] + _dot(p.astype(BF16), v)
    m_ref[...] = m_new


def _flash_result(l_ref, acc_ref):
    return acc_ref[...] * (1.0 / jnp.sum(l_ref[...], axis=-1, keepdims=True))


def _mla_body(qc_ref, kc_ref, o_ref, m_ref, l_ref, acc_ref, *, tq):
    qi = pl.program_id(1)
    q = qc_ref[0].reshape(H_A * tq, QC_W)
    _flash_init(m_ref, l_ref, acc_ref)

    def step(j, keep):
        k = kc_ref[0, pl.ds(pl.multiple_of(j * tq, tq), tq), :]
        _flash_step(q, k, k[:, :KV_RANK], keep, m_ref, l_ref, acc_ref, MLA_SCALE)

    def full_step(j, carry):
        step(j, None)
        return carry

    lax.fori_loop(0, qi, full_step, 0)
    row = lax.broadcasted_iota(I32, (H_A * tq, tq), 0)
    col = lax.broadcasted_iota(I32, (H_A * tq, tq), 1)
    step(qi, col <= (row & (tq - 1)))
    o_ref[0] = _flash_result(l_ref, acc_ref).reshape(H_A, tq, KV_RANK).astype(BF16)


def _mla_call(qc, kc, tq):
    b, _, t, _ = qc.shape
    return pl.pallas_call(
        functools.partial(_mla_body, tq=tq),
        out_shape=jax.ShapeDtypeStruct((b, H_A, t, KV_RANK), BF16),
        grid=(b, t // tq),
        in_specs=[pl.BlockSpec((1, H_A, tq, QC_W), lambda bi, i: (bi, 0, i, 0)),
                  pl.BlockSpec((1, t, QC_W), lambda bi, i: (bi, 0, 0))],
        out_specs=pl.BlockSpec((1, H_A, tq, KV_RANK), lambda bi, i: (bi, 0, i, 0)),
        scratch_shapes=[pltpu.VMEM((H_A * tq, LANES), F32), pltpu.VMEM((H_A * tq, LANES), F32),
                        pltpu.VMEM((H_A * tq, KV_RANK), F32)],
        compiler_params=_cparams(("parallel", "parallel")), name="mla_prompt",
    )(qc, kc)


def _order_key(score):
    score = jnp.where(score == 0.0, 0.0, score)
    bits = lax.bitcast_convert_type(score, I32)
    return bits ^ ((bits >> 31) & jnp.int32(0x7FFFFFFF))


def _head_rows(q):
    lane = lax.broadcasted_iota(I32, q.shape, 1)
    return jnp.concatenate([jnp.where((lane >= h * DH_B) & (lane < (h + 1) * DH_B), q, 0.0) for h in range(H_B)],
                           axis=0)


def _head_lanes(o, r):
    lane = lax.broadcasted_iota(I32, (r, o.shape[1]), 1)
    out = jnp.zeros((r, o.shape[1]), F32)
    for h in range(H_B):
        out = jnp.where((lane >= h * DH_B) & (lane < (h + 1) * DH_B), o[h * r:(h + 1) * r], out)
    return out


def _dsa_body(qih_ref, aux_ref, kib_ref, qbb_ref, kbb_ref, vbb_ref, ob_ref, skey_ref, tc_ref, m_ref, l_ref,
              acc_ref, *, tq, ck, topk, idx_bits):
    qi = pl.program_id(1)
    nk = (qi * tq) // ck + 1
    q4 = qih_ref[0].reshape(H_IDX * tq, D_IDX)
    w_t = aux_ref[0].T[AUX_WI0:AUX_WI0 + H_IDX, :] * (H_IDX ** -0.5)
    krow = lax.broadcasted_iota(I32, (ck, tq), 0)
    qpos = qi * tq + lax.broadcasted_iota(I32, (ck, tq), 1)

    def kpos_of(c):
        return c * ck + krow

    def score_chunk(c, carry):
        kc = kib_ref[0, pl.ds(pl.multiple_of(c * ck, ck), ck), :]
        d_t = _dot_nt(kc, q4) * (D_IDX ** -0.5)
        sc = jnp.zeros((ck, tq), F32)
        for h in range(H_IDX):
            sc = sc + jnp.maximum(d_t[:, h * tq:(h + 1) * tq], 0.0) * w_t[h:h + 1, :]
        skey_ref[c] = jnp.where(kpos_of(c) <= qpos, _order_key(sc), jnp.int32(INT_MIN))
        return carry

    lax.fori_loop(0, nk, score_chunk, 0)

    kf = jnp.float32(topk)

    def search(n):
        def count(pred):
            part = jnp.zeros((8, tq), F32)
            for c in range(n):
                part = part + jnp.sum(jnp.where(pred(c), 1.0, 0.0).reshape(ck // 8, 8, tq), axis=0)
            return jnp.sum(part, axis=0, keepdims=True)

        def value_step(i, t):
            cand = t + (jnp.int32(1) << (jnp.int32(31) - i))
            return jnp.where(count(lambda c: skey_ref[c] >= cand) >= kf, cand, t)

        t = lax.fori_loop(0, 32, value_step, jnp.full((1, tq), INT_MIN, I32))
        need = kf - count(lambda c: skey_ref[c] > t)

        def tie(c):
            return (skey_ref[c] == t) & (kpos_of(c) <= qpos)

        tc_ref[0:1, :] = t
        tc_ref[1:2, :] = jnp.full((1, tq), 2 ** idx_bits, I32)
        surplus = jnp.max(count(tie) - need)

        @pl.when(surplus > 0.0)
        def _():
            def index_step(i, cut):
                cand = cut + (jnp.int32(1) << (jnp.int32(idx_bits - 1) - i))
                return jnp.where(count(lambda c: tie(c) & (kpos_of(c) < cand)) < need, cand, cut)

            tc_ref[1:2, :] = lax.fori_loop(0, idx_bits, index_step, jnp.zeros((1, tq), I32))

    tc_ref[0:1, :] = jnp.full((1, tq), INT_MIN, I32)
    tc_ref[1:2, :] = jnp.zeros((1, tq), I32)
    for n in range(1, skey_ref.shape[0] + 1):
        pl.when((nk == n) & ((qi + 1) * tq > topk))(functools.partial(search, n))
    t = tc_ref[0:1, :]
    cut = tc_ref[1:2, :]

    def tie(c):
        return (skey_ref[c] == t) & (kpos_of(c) <= qpos)

    qm = _head_rows(qbb_ref[0].astype(F32)).astype(BF16)
    _flash_init(m_ref, l_ref, acc_ref)

    def attend(c, carry):
        sel_t = (kpos_of(c) <= qpos) & ((skey_ref[c] > t) | (tie(c) & (kpos_of(c) <= cut)))
        keep = jnp.where(sel_t, 1.0, 0.0).T
        keep = jnp.concatenate([keep] * H_B, axis=0) > 0.5
        rows = pl.ds(pl.multiple_of(c * ck, ck), ck)
        _flash_step(qm, kbb_ref[0, rows, :], vbb_ref[0, rows, :], keep, m_ref, l_ref, acc_ref, DH_B ** -0.5)
        return carry

    lax.fori_loop(0, nk, attend, 0)
    ob_ref[0] = _head_lanes(_flash_result(l_ref, acc_ref), tq).astype(BF16)


def _dsa_call(qih, aux, kib, qbb, kbb, vbb, tq, topk):
    b, t, _ = qbb.shape
    ck = _pick_tile(t, 256)
    idx_bits = max(1, int(np.ceil(np.log2(t))))
    whole = lambda w: pl.BlockSpec((1, t, w), lambda bi, i: (bi, 0, 0))
    tile = lambda w: pl.BlockSpec((1, tq, w), lambda bi, i: (bi, i, 0))
    return pl.pallas_call(
        functools.partial(_dsa_body, tq=tq, ck=ck, topk=topk, idx_bits=idx_bits),
        out_shape=jax.ShapeDtypeStruct((b, t, B_WIDTH), BF16),
        grid=(b, t // tq),
        in_specs=[pl.BlockSpec((1, H_IDX, tq, D_IDX), lambda bi, i: (bi, 0, i, 0)), tile(LANES), whole(D_IDX),
                  tile(B_WIDTH), whole(B_WIDTH), whole(B_WIDTH)],
        out_specs=tile(B_WIDTH),
        scratch_shapes=[pltpu.VMEM((t // ck, ck, tq), I32), pltpu.VMEM((8, tq), I32),
                        pltpu.VMEM((H_B * tq, LANES), F32),
                        pltpu.VMEM((H_B * tq, LANES), F32), pltpu.VMEM((H_B * tq, B_WIDTH), F32)],
        compiler_params=_cparams(("parallel", "parallel")), name="dsa_prompt",
    )(qih, aux, kib, qbb, kbb, vbb)


def _pool_delta(win_sums, cnts, xcur):
    lane = lax.broadcasted_iota(I32, xcur.shape, 1)
    mean = jnp.zeros(xcur.shape, F32)
    for g, w in enumerate(POOL_WINDOWS):
        in_group = (lane >= g * POOL_CH) & (lane < (g + 1) * POOL_CH)
        mean = jnp.where(in_group, win_sums[w] / cnts[w], mean)
    return mean - xcur


def _merge_tail(o_lat_heads, ob, delta, x, wuv_ref, woa_ref, wob_ref, woc_ref, wp_ref, cs_ref, g_ref, b_ref):
    oc = _dot(delta.astype(BF16), wp_ref[...]) * cs_ref[...]
    acc = _dot(ob, wob_ref[...]) + _dot(oc.astype(BF16), woc_ref[...])
    for h in range(H_A):
        oa = _dot(o_lat_heads(h), wuv_ref[h]).astype(BF16)
        acc = acc + _dot(oa, woa_ref[h])
    return _layer_norm(ALPHA * x + acc, g_ref[...], b_ref[...])


def _merge_body(ol_ref, ob_ref, xc_ref, buf_ref, x_ref, wuv_ref, woa_ref, wob_ref, woc_ref, wp_ref, cs_ref, g_ref,
                b_ref, h_ref, ext_ref, *, tm):
    i = pl.program_id(1)
    start = pl.multiple_of(i * tm, tm)

    @pl.when(i == 0)
    def _():
        ext_ref[0:HALO, :] = buf_ref[0]

    @pl.when(i > 0)
    def _():
        ext_ref[0:HALO, :] = xc_ref[0, pl.ds(start - HALO, HALO), :]

    xcur = xc_ref[0, pl.ds(start, tm), :]
    ext_ref[HALO:HALO + tm, :] = xcur
    run = xcur
    win_sums = {}
    for k in range(1, max(POOL_WINDOWS)):
        run = run + ext_ref[HALO - k:HALO - k + tm, :]
        if k + 1 in POOL_WINDOWS:
            win_sums[k + 1] = run
    pos1 = (start + 1 + lax.broadcasted_iota(I32, (tm, 1), 0)).astype(F32)
    cnts = {w: jnp.minimum(jnp.float32(w), pos1) for w in POOL_WINDOWS}
    delta = _pool_delta(win_sums, cnts, xcur)
    h_ref[0] = _merge_tail(lambda h: ol_ref[0, h], ob_ref[0], delta, x_ref[0], wuv_ref, woa_ref, wob_ref, woc_ref,
                           wp_ref, cs_ref, g_ref, b_ref)


def _merge_call(o_lat, ob, xc, buf, x, mw, tm):
    b, t, d = x.shape
    tile = lambda w: pl.BlockSpec((1, tm, w), lambda bi, i: (bi, i, 0))
    full = lambda a: pl.BlockSpec(a.shape, lambda bi, i: (0,) * a.ndim)
    wnames = ('wuv', 'woa', 'wob', 'woc', 'wp', 'cs', 'g1', 'b1')
    return pl.pallas_call(
        functools.partial(_merge_body, tm=tm),
        out_shape=jax.ShapeDtypeStruct((b, t, d), F32),
        grid=(b, t // tm),
        in_specs=[pl.BlockSpec((1, H_A, tm, KV_RANK), lambda bi, i: (bi, 0, i, 0)), tile(B_WIDTH),
                  pl.BlockSpec((1, t, C_WIDTH), lambda bi, i: (bi, 0, 0)),
                  pl.BlockSpec((1, HALO, C_WIDTH), lambda bi, i: (bi, 0, 0)), tile(d)]
                 + [full(mw[n]) for n in wnames],
        out_specs=tile(d),
        scratch_shapes=[pltpu.VMEM((HALO + tm, C_WIDTH), F32)],
        compiler_params=_cparams(("parallel", "arbitrary")), name="merge_prompt",
    )(o_lat, ob, xc, buf, x, *[mw[n] for n in wnames])


def _merge_sample_body(ol_ref, ob_ref, ext_ref, x_ref, wuv_ref, woa_ref, wob_ref, woc_ref, wp_ref, cs_ref, g_ref,
                       b_ref, h_ref, *, n_seen):
    n = ext_ref.shape[1]
    xcur = ext_ref[HALO - 1]
    run = xcur
    win_sums = {}
    for k in range(1, max(POOL_WINDOWS)):
        run = run + ext_ref[HALO - 1 - k]
        if k + 1 in POOL_WINDOWS:
            win_sums[k + 1] = run
    cnts = {w: jnp.full((n, 1), min(w, n_seen), F32) for w in POOL_WINDOWS}
    delta = _pool_delta(win_sums, cnts, xcur)
    h_ref[...] = _merge_tail(lambda h: ol_ref[h], ob_ref[...], delta, x_ref[...], wuv_ref, woa_ref, wob_ref, woc_ref,
                             wp_ref, cs_ref, g_ref, b_ref)


def _merge_sample_call(o_lat_h, ob, ext, x, mw, n_seen):
    n, d = x.shape
    wnames = ('wuv', 'woa', 'wob', 'woc', 'wp', 'cs', 'g1', 'b1')
    return pl.pallas_call(
        functools.partial(_merge_sample_body, n_seen=n_seen),
        out_shape=jax.ShapeDtypeStruct((n, d), F32), name="merge_sample",
        compiler_params=pltpu.CompilerParams(vmem_limit_bytes=VMEM_LIMIT),
    )(o_lat_h, ob, ext, x, *[mw[n_] for n_ in wnames])


def _swiglu_partial(xb, wg, wu, wd):
    a = _dot(xb, wg)
    u = _dot(xb, wu)
    hm = (a / (1.0 + jnp.exp(-a))) * u
    return _dot(hm.astype(BF16), wd)


def _ffn_body(h_ref, wg_ref, wu_ref, wd_ref, g_ref, b_ref, o_ref, xb_ref, acc_ref):
    k = pl.program_id(1)

    @pl.when(k == 0)
    def _():
        xb_ref[...] = h_ref[...].astype(BF16)
        acc_ref[...] = jnp.zeros(acc_ref.shape, F32)

    acc_ref[...] += _swiglu_partial(xb_ref[...], wg_ref[...], wu_ref[...], wd_ref[...])

    @pl.when(k == pl.num_programs(1) - 1)
    def _():
        o_ref[...] = _layer_norm(ALPHA * h_ref[...] + acc_ref[...], g_ref[...], b_ref[...])


def _ffn_call(h, wg, wu, wd, g, bta, tm, tf):
    n, d = h.shape
    f = wg.shape[1]
    return pl.pallas_call(
        _ffn_body, out_shape=jax.ShapeDtypeStruct((n, d), F32), grid=(n // tm, f // tf),
        in_specs=[pl.BlockSpec((tm, d), lambda i, k: (i, 0)), pl.BlockSpec((d, tf), lambda i, k: (0, k)),
                  pl.BlockSpec((d, tf), lambda i, k: (0, k)), pl.BlockSpec((tf, d), lambda i, k: (k, 0)),
                  pl.BlockSpec((1, d), lambda i, k: (0, 0)), pl.BlockSpec((1, d), lambda i, k: (0, 0))],
        out_specs=pl.BlockSpec((tm, d), lambda i, k: (i, 0)),
        scratch_shapes=[pltpu.VMEM((tm, d), BF16), pltpu.VMEM((tm, d), F32)],
        compiler_params=_cparams(("parallel", "arbitrary")), name="ffn_dense",
    )(h, wg, wu, wd, g, bta)


def _router_body(h_ref, r_ref, gate_ref):
    logits = jnp.dot(h_ref[...], r_ref[...], precision=lax.Precision.HIGHEST, preferred_element_type=F32)
    lane = lax.broadcasted_iota(I32, logits.shape, 1).astype(F32)
    logits = jnp.where(lane < N_EXPERTS, logits, -jnp.inf)
    v1 = jnp.max(logits, axis=-1, keepdims=True)
    i1 = jnp.min(jnp.where(logits == v1, lane, float(LANES)), axis=-1, keepdims=True)
    rest = jnp.where(lane == i1, -jnp.inf, logits)
    v2 = jnp.max(rest, axis=-1, keepdims=True)
    i2 = jnp.min(jnp.where(rest == v2, lane, float(LANES)), axis=-1, keepdims=True)
    e2 = jnp.exp(v2 - v1)
    g1 = 1.0 / (1.0 + e2)
    g2 = e2 / (1.0 + e2)
    dense = jnp.where(lane == i1, g1, jnp.where(lane == i2, g2, 0.0))
    for off, val in enumerate((i1, i2, g1, g2)):
        dense = jnp.where(lane == float(ROUTE_LANE0 + off), val, dense)
    gate_ref[...] = dense


def _router_call(h, router_pad, tm):
    n, d = h.shape
    return pl.pallas_call(
        _router_body, out_shape=jax.ShapeDtypeStruct((n, LANES), F32), grid=(n // tm,),
        in_specs=[pl.BlockSpec((tm, d), lambda i: (i, 0)), pl.BlockSpec((d, LANES), lambda i: (0, 0))],
        out_specs=pl.BlockSpec((tm, LANES), lambda i: (i, 0)),
        compiler_params=_cparams(("parallel",)), name="moe_router",
    )(h, router_pad)


def _moe_body(h_ref, gate_ref, wg_ref, wu_ref, wd_ref, g_ref, b_ref, o_ref, xb_ref, acc_ref):
    e = pl.program_id(1)
    k = pl.program_id(2)

    @pl.when((e == 0) & (k == 0))
    def _():
        xb_ref[...] = h_ref[...].astype(BF16)
        acc_ref[...] = jnp.zeros(acc_ref.shape, F32)

    gates = gate_ref[...]
    lane = lax.broadcasted_iota(I32, gates.shape, 1)
    ge = jnp.sum(jnp.where(lane == e, gates, 0.0), axis=-1, keepdims=True)
    acc_ref[...] += ge * _swiglu_partial(xb_ref[...], wg_ref[0], wu_ref[0], wd_ref[0])

    @pl.when((e == pl.num_programs(1) - 1) & (k == pl.num_programs(2) - 1))
    def _():
        o_ref[...] = _layer_norm(ALPHA * h_ref[...] + acc_ref[...], g_ref[...], b_ref[...])


def _moe_call(h, gates, wg, wu, wd, g, bta, tm, tf):
    n, d = h.shape
    ne, _, f = wg.shape
    return pl.pallas_call(
        _moe_body, out_shape=jax.ShapeDtypeStruct((n, d), F32), grid=(n // tm, ne, f // tf),
        in_specs=[pl.BlockSpec((tm, d), lambda i, e, k: (i, 0)), pl.BlockSpec((tm, LANES), lambda i, e, k: (i, 0)),
                  pl.BlockSpec((1, d, tf), lambda i, e, k: (e, 0, k)),
                  pl.BlockSpec((1, d, tf), lambda i, e, k: (e, 0, k)),
                  pl.BlockSpec((1, tf, d), lambda i, e, k: (e, k, 0)),
                  pl.BlockSpec((1, d), lambda i, e, k: (0, 0)), pl.BlockSpec((1, d), lambda i, e, k: (0, 0))],
        out_specs=pl.BlockSpec((tm, d), lambda i, e, k: (i, 0)),
        scratch_shapes=[pltpu.VMEM((tm, d), BF16), pltpu.VMEM((tm, d), F32)],
        compiler_params=_cparams(("parallel", "arbitrary", "arbitrary")), name="ffn_moe",
    )(h, gates, wg, wu, wd, g, bta)


def _route_plan(route, tm):
    n = route.shape[0]
    ea = route[:, ROUTE_LANE0:ROUTE_LANE0 + 2].astype(I32).T.reshape(-1)
    ga = route[:, ROUTE_LANE0 + 2:ROUTE_LANE0 + 4].T.reshape(-1)
    onehot = (ea[:, None] == jnp.arange(N_EXPERTS, dtype=I32)[None, :]).astype(I32)
    csum = jnp.cumsum(onehot, axis=0)
    counts = csum[-1]
    rank = jnp.sum(onehot * csum, axis=1) - 1
    tiles_e = (counts + tm - 1) // tm
    tile_end = jnp.cumsum(tiles_e)
    tile_off = tile_end - tiles_e
    pos = tile_off[ea] * tm + rank
    n_tiles = (2 * n) // tm + N_EXPERTS
    pair = jnp.arange(2 * n, dtype=I32)
    tok_of = jnp.zeros((n_tiles * tm,), I32).at[pos].set(pair % n)
    spare = 2 * n + jnp.arange(n_tiles * tm, dtype=I32) % tm
    dst_of = spare.at[pos].set(pair)
    gate_of = jnp.zeros((n_tiles * tm,), F32).at[pos].set(ga)
    tile_ids = jnp.arange(n_tiles, dtype=I32)
    te = jnp.minimum(jnp.sum((tile_ids[:, None] >= tile_end[None, :]).astype(I32), axis=1), N_EXPERTS - 1)
    nv = jnp.clip(counts[te] - (tile_ids - tile_off[te]) * tm, 0, tm)
    return (tok_of.reshape(n_tiles, 1, tm), dst_of.reshape(n_tiles, 1, tm), gate_of.reshape(n_tiles * tm, 1),
            te.astype(I32), nv.astype(I32))


def _moe_routed_body(te_ref, nv_ref, tokc_ref, tokn_ref, dst_ref, gate_ref, h_hbm, wg_ref, wu_ref, wd_ref, y_hbm,
                     xbuf, xb_ref, acc_ref, ybuf, gsem, ssem, *, tm):
    i = pl.program_id(0)
    k = pl.program_id(1)
    n_t = pl.num_programs(0)
    slot = i % 2

    def gather_copy(tok_ref, r, buf_slot):
        return pltpu.make_async_copy(h_hbm.at[pl.ds(tok_ref[0, 0, r], 1), :], xbuf.at[buf_slot, pl.ds(r, 1), :],
                                     gsem.at[buf_slot])

    def scatter_copy(r):
        return pltpu.make_async_copy(ybuf.at[pl.ds(r, 1), :], y_hbm.at[pl.ds(dst_ref[0, 0, r], 1), :], ssem.at[0])

    def start_rows(copy_of):
        def body(r, carry):
            copy_of(r).start()
            return carry
        lax.fori_loop(0, tm, body, 0, unroll=8)

    def wait_gather(buf_slot):
        pltpu.make_async_copy(h_hbm.at[pl.ds(0, tm), :], xbuf.at[buf_slot], gsem.at[buf_slot]).wait()

    def wait_scatter():
        pltpu.make_async_copy(ybuf, y_hbm.at[pl.ds(0, tm), :], ssem.at[0]).wait()

    @pl.when(k == 0)
    def _():
        @pl.when(i == 0)
        def _():
            start_rows(lambda r: gather_copy(tokc_ref, r, 0))

        wait_gather(slot)

        @pl.when(i + 1 < n_t)
        def _():
            start_rows(lambda r: gather_copy(tokn_ref, r, 1 - slot))

        xb_ref[...] = xbuf[slot].astype(BF16)
        acc_ref[...] = jnp.zeros(acc_ref.shape, F32)

    @pl.when(nv_ref[i] > 0)
    def _():
        acc_ref[...] += _swiglu_partial(xb_ref[...], wg_ref[0], wu_ref[0], wd_ref[0])

    @pl.when(k == pl.num_programs(1) - 1)
    def _():
        @pl.when(i > 0)
        def _():
            wait_scatter()

        ybuf[...] = acc_ref[...] * gate_ref[...]
        start_rows(scatter_copy)

        @pl.when(i == n_t - 1)
        def _():
            wait_scatter()


def _moe_routed_call(h, route, wg, wu, wd, tm, tf):
    n, d = h.shape
    f = wg.shape[2]
    tok_of, dst_of, gate_of, te, nv = _route_plan(route, tm)
    n_tiles = tok_of.shape[0]
    smem_row = lambda fn: pl.BlockSpec((1, 1, tm), fn, memory_space=pltpu.SMEM)
    grid_spec = pltpu.PrefetchScalarGridSpec(
        num_scalar_prefetch=2, grid=(n_tiles, f // tf),
        in_specs=[smem_row(lambda i, k, te_, nv_: (i, 0, 0)),
                  smem_row(lambda i, k, te_, nv_: (jnp.minimum(i + 1, n_tiles - 1), 0, 0)),
                  smem_row(lambda i, k, te_, nv_: (i, 0, 0)),
                  pl.BlockSpec((tm, 1), lambda i, k, te_, nv_: (i, 0)),
                  pl.BlockSpec(memory_space=pl.ANY),
                  pl.BlockSpec((1, d, tf), lambda i, k, te_, nv_: (te_[i], 0, k)),
                  pl.BlockSpec((1, d, tf), lambda i, k, te_, nv_: (te_[i], 0, k)),
                  pl.BlockSpec((1, tf, d), lambda i, k, te_, nv_: (te_[i], k, 0))],
        out_specs=pl.BlockSpec(memory_space=pl.ANY),
        scratch_shapes=[pltpu.VMEM((2, tm, d), F32), pltpu.VMEM((tm, d), BF16), pltpu.VMEM((tm, d), F32),
                        pltpu.VMEM((tm, d), F32), pltpu.SemaphoreType.DMA((2,)), pltpu.SemaphoreType.DMA((1,))])
    return pl.pallas_call(
        functools.partial(_moe_routed_body, tm=tm),
        out_shape=jax.ShapeDtypeStruct((2 * n + tm, d), F32), grid_spec=grid_spec,
        compiler_params=_cparams(("arbitrary", "arbitrary")), name="ffn_moe_routed",
    )(te, nv, tok_of, tok_of, dst_of, gate_of, h, wg, wu, wd)


def _combine_body(h_ref, y0_ref, y1_ref, g_ref, b_ref, o_ref):
    o_ref[...] = _layer_norm(ALPHA * h_ref[...] + (y0_ref[...] + y1_ref[...]), g_ref[...], b_ref[...])


def _combine_call(h, y2, g, bta, tm):
    n, d = h.shape
    return pl.pallas_call(
        _combine_body, out_shape=jax.ShapeDtypeStruct((n, d), F32), grid=(n // tm,),
        in_specs=[pl.BlockSpec((tm, d), lambda i: (i, 0)), pl.BlockSpec((tm, d), lambda i: (i, 0)),
                  pl.BlockSpec((tm, d), lambda i: (i + n // tm, 0)),
                  pl.BlockSpec((1, d), lambda i: (0, 0)), pl.BlockSpec((1, d), lambda i: (0, 0))],
        out_specs=pl.BlockSpec((tm, d), lambda i: (i, 0)),
        compiler_params=_cparams(("parallel",)), name="moe_combine",
    )(h, y2, y2, g, bta)


def _mla_sample_body(pt_ref, q_ref, knew_ref, *refs, npg):
    ck_refs = refs[:npg]
    kr_refs = refs[npg:2 * npg]
    o_ref, m_ref, l_ref, acc_ref = refs[2 * npg:]
    s_i = pl.program_id(1)

    @pl.when(s_i == 0)
    def _():
        m_ref[...] = jnp.full(m_ref.shape, -jnp.inf, F32)
        l_ref[...] = jnp.zeros(l_ref.shape, F32)
        acc_ref[...] = jnp.zeros(acc_ref.shape, F32)

    q = q_ref[0]
    q_rope = q[:, KV_RANK + ROPE_LANE0:KV_RANK + ROPE_LANE0 + ROPE_A]
    ck = jnp.concatenate([r[0, 0] for r in ck_refs], axis=0).astype(BF16)
    kr_t = jnp.concatenate([r[0, 0] for r in kr_refs], axis=1).astype(BF16)
    s = (_dot_nt(q[:, :KV_RANK], ck) + _dot(q_rope, kr_t)) * MLA_SCALE
    m_prev = m_ref[...]
    m_new = jnp.maximum(m_prev, jnp.max(s, axis=-1, keepdims=True))
    alpha = jnp.exp(m_prev - m_new)
    p = jnp.exp(s - m_new)
    l_ref[...] = alpha * l_ref[...] + jnp.sum(p, axis=-1, keepdims=True)
    acc_ref[...] = alpha * acc_ref[...] + _dot(p.astype(BF16), ck)
    m_ref[...] = m_new

    @pl.when(s_i == pl.num_programs(1) - 1)
    def _():
        kn = knew_ref[0].astype(F32)
        s_n = jnp.sum(q.astype(F32) * kn, axis=-1, keepdims=True) * MLA_SCALE
        m_old = m_ref[...]
        m_fin = jnp.maximum(m_old, s_n)
        a = jnp.exp(m_old - m_fin)
        p_n = jnp.exp(s_n - m_fin)
        l_fin = a * l_ref[...] + p_n
        acc = a * acc_ref[...] + p_n * kn[:, :KV_RANK]
        o_ref[0] = (acc / l_fin).astype(BF16)


def _mla_sample_call(page_table, q, knew, cache_ckv, cache_kr, layer, npg):
    n, n_pages = page_table.shape
    steps = n_pages // npg
    ck_specs = [pl.BlockSpec((1, 1, PAGE_SIZE, KV_RANK),
                             functools.partial(lambda bi, s, pt, j: (layer, pt[bi, s * npg + j], 0, 0), j=j))
                for j in range(npg)]
    kr_specs = [pl.BlockSpec((1, 1, ROPE_A, PAGE_SIZE),
                             functools.partial(lambda bi, s, pt, j: (layer, pt[bi, s * npg + j], 0, 0), j=j))
                for j in range(npg)]
    grid_spec = pltpu.PrefetchScalarGridSpec(
        num_scalar_prefetch=1, grid=(n, steps),
        in_specs=[pl.BlockSpec((1, H_A, QC_W), lambda bi, s, pt: (bi, 0, 0)),
                  pl.BlockSpec((1, 1, QC_W), lambda bi, s, pt: (bi, 0, 0))] + ck_specs + kr_specs,
        out_specs=pl.BlockSpec((1, H_A, KV_RANK), lambda bi, s, pt: (bi, 0, 0)),
        scratch_shapes=[pltpu.VMEM((H_A, 1), F32), pltpu.VMEM((H_A, 1), F32), pltpu.VMEM((H_A, KV_RANK), F32)])
    return pl.pallas_call(
        functools.partial(_mla_sample_body, npg=npg),
        out_shape=jax.ShapeDtypeStruct((n, H_A, KV_RANK), BF16), grid_spec=grid_spec,
        compiler_params=_cparams(("parallel", "arbitrary")), name="mla_sample",
    )(page_table, q, knew, *([cache_ckv] * npg), *([cache_kr] * npg))


def _idx_sample_body(pt_ref, q_ref, w_ref, *refs, npg):
    ki_refs = refs[:npg]
    o_ref = refs[npg]
    ki_t = jnp.concatenate([r[0, 0] for r in ki_refs], axis=1).astype(BF16)
    dots = _dot(q_ref[0], ki_t) * (D_IDX ** -0.5)
    w = w_ref[0] * (H_IDX ** -0.5)
    o_ref[0] = jnp.sum(jnp.maximum(dots, 0.0) * w, axis=0, keepdims=True)


def _idx_sample_call(page_table, qi, wi, cache_kidx, layer, npg):
    n, n_pages = page_table.shape
    steps = n_pages // npg
    ki_specs = [pl.BlockSpec((1, 1, D_IDX, PAGE_SIZE),
                             functools.partial(lambda bi, s, pt, j: (layer, pt[bi, s * npg + j], 0, 0), j=j))
                for j in range(npg)]
    grid_spec = pltpu.PrefetchScalarGridSpec(
        num_scalar_prefetch=1, grid=(n, steps),
        in_specs=[pl.BlockSpec((1, H_IDX, D_IDX), lambda bi, s, pt: (bi, 0, 0)),
                  pl.BlockSpec((1, H_IDX, 1), lambda bi, s, pt: (bi, 0, 0))] + ki_specs,
        out_specs=pl.BlockSpec((1, 1, npg * PAGE_SIZE), lambda bi, s, pt: (bi, 0, s)))
    return pl.pallas_call(
        functools.partial(_idx_sample_body, npg=npg),
        out_shape=jax.ShapeDtypeStruct((n, 1, n_pages * PAGE_SIZE), F32), grid_spec=grid_spec,
        compiler_params=_cparams(("parallel", "arbitrary")), name="idx_sample",
    )(page_table, qi, wi, *([cache_kidx] * npg))


def _select_sample_body(sp_ref, qi_ref, w_ref, kin_ref, sel_ref, *, past, topk, idx_bits):
    n = sp_ref.shape[0]
    kn = kin_ref[...].astype(F32)
    s_new = jnp.zeros((n, 1), F32)
    for h in range(H_IDX):
        d = jnp.sum(qi_ref[h].astype(F32) * kn, axis=-1, keepdims=True) * (D_IDX ** -0.5)
        s_new = s_new + jnp.maximum(d, 0.0) * (w_ref[:, h:h + 1] * (H_IDX ** -0.5))
    lane = lax.broadcasted_iota(I32, (n, LANES), 1)
    tail = jnp.where(lane == 0, s_new, 0.0)
    score = jnp.concatenate([sp_ref[...], tail], axis=-1)
    kpos = lax.broadcasted_iota(I32, score.shape, 1)
    sel_ref[...] = jnp.where(_topk_mask(score, kpos <= past, kpos, topk, idx_bits), 1.0, 0.0)


def _select_sample_call(score_past, qi, wi, ki_new, topk):
    n, past = score_past.shape
    idx_bits = int(np.ceil(np.log2(past + LANES)))
    return pl.pallas_call(
        functools.partial(_select_sample_body, past=past, topk=topk, idx_bits=idx_bits),
        out_shape=jax.ShapeDtypeStruct((n, past + LANES), F32),
        compiler_params=pltpu.CompilerParams(vmem_limit_bytes=VMEM_LIMIT), name="select_sample",
    )(score_past, qi, wi, ki_new)


def _dsa_sample_body(pt_ref, q_ref, sel_ref, seln_ref, kn_ref, vn_ref, *refs, npg):
    k_refs = refs[:npg]
    v_refs = refs[npg:2 * npg]
    o_ref, m_ref, l_ref, acc_ref = refs[2 * npg:]
    s_i = pl.program_id(1)

    @pl.when(s_i == 0)
    def _():
        _flash_init(m_ref, l_ref, acc_ref)

    qcol = q_ref[0]
    keep = sel_ref[0] > 0.5
    scale = (DH_B ** -0.5) * LOG2E
    for h in range(H_B):
        qh = qcol[h * DH_B:(h + 1) * DH_B]
        k_t = jnp.concatenate([r[0, 0, h] for r in k_refs], axis=1)
        s = jnp.sum(k_t * qh, axis=0, keepdims=True) * scale
        s = jnp.where(keep, s, -jnp.inf)
        m_prev = m_ref[h]
        m_new = jnp.maximum(m_prev, jnp.max(s, axis=-1, keepdims=True))
        m_use = jnp.where(m_new == -jnp.inf, 0.0, m_new)
        alpha = jnp.exp2(m_prev - m_use)
        p = jnp.exp2(s - _lane_tile(m_use, s.shape[1]))
        v_t = jnp.concatenate([r[0, 0, h] for r in v_refs], axis=1)
        l_ref[h] = alpha * l_ref[h] + _lane_fold(p)
        acc_ref[h] = alpha * acc_ref[h] + _lane_fold(v_t * p)
        m_ref[h] = m_new

    @pl.when(s_i == pl.num_programs(1) - 1)
    def _():
        new_kept = seln_ref[0][:, 0:1] > 0.5
        for h in range(H_B):
            rows = slice(h * DH_B, (h + 1) * DH_B)
            s_n = jnp.sum(kn_ref[0][rows] * qcol[rows], axis=0, keepdims=True) * scale
            s_n = jnp.where(new_kept, s_n, -jnp.inf)
            m_prev = m_ref[h]
            m_new = jnp.maximum(m_prev, s_n)
            m_use = jnp.where(m_new == -jnp.inf, 0.0, m_new)
            alpha = jnp.exp2(m_prev - m_use)
            p_n = jnp.exp2(s_n - m_use)[:, 0:1]
            l_tot = jnp.sum(alpha * l_ref[h], axis=-1, keepdims=True) + p_n
            acc = jnp.sum(alpha * acc_ref[h], axis=-1, keepdims=True) + p_n * vn_ref[0][rows]
            o_ref[0, rows, :] = acc * (1.0 / l_tot)


def _dsa_sample_call(page_table, q_col, sel, kn_col, vn_col, cache_k_t, cache_v_t, layer, npg):
    n, n_pages = page_table.shape
    steps = n_pages // npg
    page_specs = [pl.BlockSpec((1, 1, H_B, DH_B, PAGE_SIZE),
                               functools.partial(lambda bi, s, pt, j: (layer, pt[bi, s * npg + j], 0, 0, 0), j=j))
                  for j in range(npg)]
    col_spec = pl.BlockSpec((1, B_WIDTH, 1), lambda bi, s, pt: (bi, 0, 0))
    grid_spec = pltpu.PrefetchScalarGridSpec(
        num_scalar_prefetch=1, grid=(n, steps),
        in_specs=[col_spec,
                  pl.BlockSpec((1, 1, npg * PAGE_SIZE), lambda bi, s, pt: (bi, 0, s)),
                  pl.BlockSpec((1, 1, LANES), lambda bi, s, pt: (bi, 0, n_pages)),
                  col_spec, col_spec] + page_specs + page_specs,
        out_specs=col_spec,
        scratch_shapes=[pltpu.VMEM((H_B, 1, LANES), F32), pltpu.VMEM((H_B, 1, LANES), F32),
                        pltpu.VMEM((H_B, DH_B, LANES), F32)])
    return pl.pallas_call(
        functools.partial(_dsa_sample_body, npg=npg),
        out_shape=jax.ShapeDtypeStruct((n, B_WIDTH, 1), F32), grid_spec=grid_spec,
        compiler_params=_cparams(("parallel", "arbitrary")), name="dsa_sample",
    )(page_table, q_col, sel, sel, kn_col, vn_col, *([cache_k_t] * npg), *([cache_v_t] * npg))


def _rot_cols(w, head_dim):
    k, n = w.shape
    wh = w.reshape(k, n // head_dim, 2, head_dim // 2)
    return jnp.concatenate([-wh[:, :, 1], wh[:, :, 0]], axis=-1).reshape(k, n)


def _prep_layer(w_in, a_q_norm, a_kv_norm, a_w_uq, a_w_uk, a_w_uv, c_w_pool, c_scale, w_out, ln_g, ln_b):
    offs = np.cumsum((0,) + IN_SPLITS)
    cq, ckv, kr, qb, kb, vb, qi, ki, wi, xc = [w_in[:, offs[i]:offs[i + 1]] for i in range(len(IN_SPLITS))]
    d = w_in.shape[0]
    zeros = lambda n: jnp.zeros((d, n), F32)
    grp_a = jnp.concatenate([ki, kr, wi, zeros(LANES - D_IDX - ROPE_A - H_IDX)], axis=1)
    grp_ar = jnp.concatenate([_rot_cols(ki, D_IDX), _rot_cols(kr, ROPE_A), zeros(LANES - D_IDX - ROPE_A)], axis=1)
    w1 = jnp.concatenate([cq, ckv, qb, _rot_cols(qb, DH_B), kb, _rot_cols(kb, DH_B), vb, qi, _rot_cols(qi, D_IDX),
                          xc, grp_a, grp_ar], axis=1).astype(BF16)
    uq = a_w_uq.reshape(Q_RANK, H_A, NOPE_A + ROPE_A)
    nope = jnp.pad(uq[:, :, :NOPE_A], ((0, 0), (0, 0), (0, LANES - NOPE_A)))
    rope = uq[:, :, NOPE_A:]
    rope_rot = jnp.concatenate([-rope[..., ROPE_A // 2:], rope[..., :ROPE_A // 2]], axis=-1)
    pad_rope = lambda r: jnp.pad(r, ((0, 0), (0, 0), (ROPE_LANE0, LANES - ROPE_LANE0 - ROPE_A)))
    wq = jnp.concatenate([nope.reshape(Q_RANK, -1), pad_rope(rope).reshape(Q_RANK, -1),
                          pad_rope(rope_rot).reshape(Q_RANK, -1)], axis=1).astype(BF16)
    wuk = jnp.pad(jnp.transpose(a_w_uk, (1, 2, 0)), ((0, 0), (0, LANES - NOPE_A), (0, 0))).astype(BF16)
    proj_w = dict(w1=w1, wq=wq, wuk=wuk, gq=a_q_norm.reshape(1, -1), gkv=a_kv_norm.reshape(1, -1))
    wuv = jnp.pad(jnp.transpose(a_w_uv, (1, 0, 2)), ((0, 0), (0, 0), (0, LANES - V_A))).astype(BF16)
    woa = jnp.pad(w_out[:A_WIDTH].reshape(H_A, V_A, -1), ((0, 0), (0, LANES - V_A), (0, 0))).astype(BF16)
    wp = jnp.zeros((C_WIDTH, C_WIDTH), F32)
    for g in range(len(POOL_WINDOWS)):
        wp = wp.at[g * POOL_CH:(g + 1) * POOL_CH, g * POOL_CH:(g + 1) * POOL_CH].set(c_w_pool[g])
    merge_w = dict(wuv=wuv, woa=woa, wob=w_out[A_WIDTH:A_WIDTH + B_WIDTH].astype(BF16),
                   woc=w_out[A_WIDTH + B_WIDTH:].astype(BF16), wp=wp.astype(BF16), cs=c_scale.reshape(1, -1),
                   g1=ln_g.reshape(1, -1), b1=ln_b.reshape(1, -1))
    return proj_w, merge_w


def _rope_tables(pos):
    posf = pos.astype(F32)[:, None]

    def cs(dim):
        inv = ROPE_THETA ** (-jnp.arange(0, dim, 2, dtype=F32) / dim)
        ang = posf * inv[None, :]
        c, s = jnp.cos(ang), jnp.sin(ang)
        return jnp.concatenate([c, c], -1), jnp.concatenate([s, s], -1)

    c64, s64 = cs(DH_B)
    c32, s32 = cs(ROPE_A)
    t = pos.shape[0]
    z = lambda n: jnp.zeros((t, n), F32)
    c_pair, s_pair = jnp.concatenate([c64, c64], -1), jnp.concatenate([s64, s64], -1)
    c_a = jnp.concatenate([c64, c32, jnp.ones((t, H_IDX), F32), z(LANES - D_IDX - ROPE_A - H_IDX)], -1)
    s_a = jnp.concatenate([s64, s32, z(LANES - D_IDX - ROPE_A)], -1)
    c_q = jnp.concatenate([z(ROPE_LANE0), c32, z(LANES - ROPE_LANE0 - ROPE_A)], -1)
    s_q = jnp.concatenate([z(ROPE_LANE0), s32, z(LANES - ROPE_LANE0 - ROPE_A)], -1)
    return c_pair, s_pair, c_a, s_a, c_q, s_q


def _aux_split(aux):
    return aux[..., D_IDX:D_IDX + ROPE_A], aux[..., :D_IDX]


def _mixers_prompt(x, proj_w, merge_w, tabs, tiles):
    b, t, _ = x.shape
    (qc, kc, ckv_n, kb, vb, aux, kbb, vbb, kib, qbb, qih, xc) = _proj_call(x, proj_w, tabs, tiles['proj'])
    o_lat = _mla_call(qc, kc, tiles['mla'])
    ob = _dsa_call(qih, aux, kib, qbb, kbb, vbb, tiles['dsa'], min(TOPK_MAX, t // 4))
    buf = jnp.zeros((b, HALO, C_WIDTH), F32)
    h = _merge_call(o_lat, ob, xc, buf, x, merge_w, tiles['merge'])
    kr_r, ki_r = _aux_split(aux)
    state = (ckv_n, kr_r, kb.reshape(b, t, H_B, DH_B), vb.reshape(b, t, H_B, DH_B), ki_r, xc[:, t - POOL_BUF:])
    return h, state


def _mixers_sample(x, proj_w, merge_w, tabs, caches, state_pool, page_table, layer, npg):
    n = x.shape[0]
    c_ckv, c_kr, c_k, c_v, c_kidx = caches
    (qc, kc, ckv_n, kb, vb, aux, kbb, vbb, kib, qbb, qih, xc) = _proj_call(x.reshape(1, n, -1), proj_w, tabs, n)
    q = jnp.transpose(qc[0], (1, 0, 2))
    o_lat = _mla_sample_call(page_table, q, kc[0][:, None, :], c_ckv, c_kr, layer, npg)
    qi = jnp.transpose(qih[0], (1, 0, 2))
    wi = aux[0][:, AUX_WI0:AUX_WI0 + H_IDX]
    score_past = _idx_sample_call(page_table, qi, wi[:, :, None], c_kidx, layer, npg)[:, 0, :]
    past = score_past.shape[1]
    sel = _select_sample_call(score_past, qih[0], wi, kib[0], min(TOPK_MAX, (past + 1) // 4))
    col = lambda a: a.astype(F32)[:, :, None]
    ob = _dsa_sample_call(page_table, col(qbb[0]), sel[:, None, :], col(kb[0]), col(vb[0]), c_k, c_v, layer,
                          max(1, npg // 2))[:, :, 0].astype(BF16)
    ext = jnp.concatenate([state_pool, xc[0][:, None, :]], axis=1)
    h = _merge_sample_call(jnp.transpose(o_lat, (1, 0, 2)), ob, jnp.transpose(ext, (1, 0, 2)), x, merge_w, past + 1)
    kr_r, ki_r = _aux_split(aux[0])
    state = (ckv_n[0][:, None], kr_r[:, None], kb[0].reshape(n, 1, H_B, DH_B), vb[0].reshape(n, 1, H_B, DH_B),
             ki_r[:, None], ext[:, HALO - POOL_BUF:])
    return h, state


def _pick_tile(n, pref):
    t = min(n, pref)
    while n % t:
        t //= 2
    return t


def kernel(x_prompt, x_sample, cache_a_ckv, cache_a_krope, cache_b_k, cache_b_v, cache_b_kidx, state_pool, page_table, w_in, a_q_norm, a_kv_norm, a_w_uq, a_w_uk, a_w_uv, c_w_pool, c_scale, w_out, ln1_g, ln1_b, ln2_g, ln2_b, ffn_w_gate, ffn_w_up, ffn_w_down, moe_router, moe_w_gate, moe_w_up, moe_w_down):
    bsz, seq, d = x_prompt.shape
    n_dec = x_sample.shape[0]
    depth = w_in.shape[0]
    n_pages = page_table.shape[1]
    past = n_pages * PAGE_SIZE
    caches = (cache_a_ckv, jnp.transpose(cache_a_krope, (0, 1, 3, 2)), jnp.transpose(cache_b_k, (0, 1, 3, 4, 2)),
              jnp.transpose(cache_b_v, (0, 1, 3, 4, 2)), jnp.transpose(cache_b_kidx, (0, 1, 3, 2)))
    tiles = dict(proj=_pick_tile(seq, 512), mla=_pick_tile(seq, 256), dsa=_pick_tile(seq, 128),
                 merge=_pick_tile(seq, 512))
    npg = _pick_tile(n_pages, 16)
    tabs_p = _rope_tables(jnp.arange(seq, dtype=I32))
    tabs_s = _rope_tables(jnp.full((n_dec,), past, I32))
    hp = x_prompt
    hs = x_sample.reshape(n_dec, d)
    new_p, new_s = [], []
    for l in range(depth):
        proj_w, merge_w = _prep_layer(w_in[l], a_q_norm[l], a_kv_norm[l], a_w_uq[l], a_w_uk[l], a_w_uv[l],
                                      c_w_pool[l], c_scale[l], w_out[l], ln1_g[l], ln1_b[l])
        hp, st_p = _mixers_prompt(hp, proj_w, merge_w, tabs_p, tiles)
        hs, st_s = _mixers_sample(hs, proj_w, merge_w, tabs_s, caches, state_pool[l], page_table, l, npg)
        g2, b2 = ln2_g[l].reshape(1, -1), ln2_b[l].reshape(1, -1)
        hp2 = hp.reshape(bsz * seq, d)
        j = l // 2
        tm = _pick_tile(bsz * seq, 1024)
        if l % 2 == 0:
            wg, wu, wd = ffn_w_gate[j].astype(BF16), ffn_w_up[j].astype(BF16), ffn_w_down[j].astype(BF16)
            hp2 = _ffn_call(hp2, wg, wu, wd, g2, b2, tm, 512)
            hs = _ffn_call(hs, wg, wu, wd, g2, b2, n_dec, 512)
        else:
            wg, wu, wd = moe_w_gate[j].astype(BF16), moe_w_up[j].astype(BF16), moe_w_down[j].astype(BF16)
            router = jnp.pad(moe_router[j], ((0, 0), (0, LANES - N_EXPERTS)))
            y2 = _moe_routed_call(hp2, _router_call(hp2, router, tm), wg, wu, wd, _pick_tile(bsz * seq, MOE_ROW_TILE), 512)
            hp2 = _combine_call(hp2, y2, g2, b2, tm)
            hs = _moe_call(hs, _router_call(hs, router, n_dec), wg, wu, wd, g2, b2, n_dec, 512)
        hp = hp2.reshape(bsz, seq, d)
        new_p.append(st_p)
        new_s.append(st_s)
    outs_p = [jnp.stack(a) for a in zip(*new_p)]
    outs_s = [jnp.stack(a) for a in zip(*new_s)]
    return (hp, hs.reshape(n_dec, 1, d), *outs_p, *outs_s)
```

```python
import functools

import numpy as np
import jax
import jax.numpy as jnp
from jax import lax
from jax.experimental import pallas as pl
from jax.experimental.pallas import tpu as pltpu

F32 = jnp.float32
BF16 = jnp.bfloat16
I32 = jnp.int32

D_MODEL = 1024
DEPTH = 2
PAGE_SIZE = 128
H_A, NOPE_A, ROPE_A, V_A = 8, 64, 32, 64
Q_RANK, KV_RANK = 256, 256
MLA_SCALE = (NOPE_A + ROPE_A) ** -0.5
H_B, DH_B = 4, 64
H_IDX, D_IDX = 4, 64
TOPK_MAX = 256
POOL_WINDOWS = (2, 4, 8, 16)
POOL_CH = 64
C_WIDTH = len(POOL_WINDOWS) * POOL_CH
POOL_BUF = max(POOL_WINDOWS) - 1
A_WIDTH = H_A * V_A
B_WIDTH = H_B * DH_B
IN_SPLITS = (Q_RANK, KV_RANK, ROPE_A, B_WIDTH, B_WIDTH, B_WIDTH, H_IDX * D_IDX, D_IDX, H_IDX, C_WIDTH)
D_FF = 3584
N_EXPERTS = 8
ROPE_THETA = 10000.0
ALPHA = (2 * DEPTH) ** 0.25
LN_EPS = 1e-5
RMS_EPS = 1e-6

LANES = 128
HALO = 16
QC_W = KV_RANK + LANES
ROPE_LANE0 = 64
AUX_WI0 = 96
INT_MIN = -(2 ** 31)
LOG2E = 1.4426950408889634
ROUTE_LANE0 = N_EXPERTS
MOE_ROW_TILE = 512
VMEM_LIMIT = 56 * 1024 * 1024


def _cparams(sem):
    return pltpu.CompilerParams(dimension_semantics=sem, vmem_limit_bytes=VMEM_LIMIT)


def _rms(x, g):
    return x * lax.rsqrt(jnp.mean(x * x, -1, keepdims=True) + RMS_EPS) * g


def _layer_norm(x, g, b):
    mu = jnp.mean(x, -1, keepdims=True)
    xc = x - mu
    var = jnp.mean(xc * xc, -1, keepdims=True)
    return xc * lax.rsqrt(var + LN_EPS) * g + b


def _dot(a, b):
    return jnp.dot(a, b, preferred_element_type=F32)


def _dot_nt(a, b):
    return lax.dot_general(a, b, (((1,), (1,)), ((), ())), preferred_element_type=F32)


def _count(mask):
    return jnp.sum(jnp.where(mask, 1.0, 0.0), axis=-1, keepdims=True)


def _key_to_float(key):
    return lax.bitcast_convert_type(key ^ ((key >> 31) & jnp.int32(0x7FFFFFFF)), F32)


def _kth_to_float(key):
    return jnp.where(key == jnp.int32(INT_MIN), -jnp.inf, _key_to_float(key))


def _topk_mask(score, valid, kpos, k, idx_bits):
    score = jnp.where(valid, score, -jnp.inf)
    rows = score.shape[0]
    kf = jnp.float32(k)

    def value_step(i, t):
        cand = t + (jnp.int32(1) << (jnp.int32(31) - i))
        return jnp.where(_count(score >= _key_to_float(cand)) >= kf, cand, t)

    t = _kth_to_float(lax.fori_loop(0, 32, value_step, jnp.full((rows, 1), INT_MIN, I32)))
    gt = score > t
    eq = (score == t) & valid
    need = kf - _count(gt)

    def index_step(i, c):
        cand = c + (jnp.int32(1) << (jnp.int32(idx_bits - 1) - i))
        return jnp.where(_count(eq & (kpos < cand)) < need, cand, c)

    c = lax.fori_loop(0, idx_bits, index_step, jnp.zeros((rows, 1), I32))
    return valid & (gt | (eq & (kpos <= c)))


_G_CQ, _G_CKV, _G_QB, _G_QBR, _G_KB, _G_KBR, _G_VB, _G_QI, _G_QIR, _G_XC = [256 * i for i in range(10)]
_G_A = 2560
_G_AR = 2688
W1_COLS = 2816


def _proj_body(x_ref, w1_ref, wq_ref, wuk_ref, gq_ref, gkv_ref, c64_ref, s64_ref, ca_ref, sa_ref, cq_ref, sq_ref,
               qc_ref, kc_ref, ckv_ref, kb_ref, vb_ref, aux_ref, kbb_ref, vbb_ref, kib_ref, qbb_ref, qih_ref,
               xc_ref):
    x = x_ref[0].astype(BF16)

    def proj(lo, width):
        return _dot(x, w1_ref[:, lo:lo + width])

    c64 = c64_ref[...]
    s64 = s64_ref[...]
    c256 = jnp.concatenate([c64, c64], axis=-1)
    s256 = jnp.concatenate([s64, s64], axis=-1)

    ckv_n = _rms(proj(_G_CKV, 256), gkv_ref[...])
    ckv_ref[0] = ckv_n

    qb = proj(_G_QB, 256) * c256 + proj(_G_QBR, 256) * s256
    qbb_ref[0] = qb.astype(BF16)
    kb = proj(_G_KB, 256) * c256 + proj(_G_KBR, 256) * s256
    kb_ref[0] = kb
    kbb_ref[0] = kb.astype(BF16)
    vb = proj(_G_VB, 256)
    vb_ref[0] = vb
    vbb_ref[0] = vb.astype(BF16)
    qi = proj(_G_QI, 256) * c256 + proj(_G_QIR, 256) * s256
    for h in range(H_IDX):
        qih_ref[0, h] = qi[:, h * D_IDX:(h + 1) * D_IDX].astype(BF16)
    xc_ref[0] = proj(_G_XC, 256)

    aux = proj(_G_A, LANES) * ca_ref[...] + proj(_G_AR, LANES) * sa_ref[...]
    aux_ref[0] = aux
    kib_ref[0] = aux[:, :D_IDX].astype(BF16)
    lane = lax.broadcasted_iota(I32, aux.shape, 1)
    kr_pad = jnp.where((lane >= ROPE_LANE0) & (lane < ROPE_LANE0 + ROPE_A), aux, 0.0)
    kc_ref[0] = jnp.concatenate([ckv_n, kr_pad], axis=-1).astype(BF16)

    cqn = _rms(proj(_G_CQ, 256), gq_ref[...]).astype(BF16)
    cq_t = cq_ref[...]
    sq_t = sq_ref[...]
    for h in range(H_A):
        q_nope = _dot(cqn, wq_ref[:, h * LANES:(h + 1) * LANES]).astype(BF16)
        q_lat = _dot(q_nope, wuk_ref[h])
        q_rope = (_dot(cqn, wq_ref[:, (H_A + h) * LANES:(H_A + h + 1) * LANES]) * cq_t
                  + _dot(cqn, wq_ref[:, (2 * H_A + h) * LANES:(2 * H_A + h + 1) * LANES]) * sq_t)
        qc_ref[0, h] = jnp.concatenate([q_lat, q_rope], axis=-1).astype(BF16)


def _proj_call(x, wts, tabs, tm):
    b, t, d = x.shape
    grid = (b, t // tm)
    tok = lambda w, dt: jax.ShapeDtypeStruct((b, t, w), dt)
    out_shape = (
        jax.ShapeDtypeStruct((b, H_A, t, QC_W), BF16),
        tok(QC_W, BF16),
        tok(KV_RANK, F32),
        tok(B_WIDTH, F32), tok(B_WIDTH, F32),
        tok(LANES, F32),
        tok(B_WIDTH, BF16), tok(B_WIDTH, BF16),
        tok(D_IDX, BF16),
        tok(B_WIDTH, BF16),
        jax.ShapeDtypeStruct((b, H_IDX, t, D_IDX), BF16),
        tok(C_WIDTH, F32),
    )
    tokspec = lambda w: pl.BlockSpec((1, tm, w), lambda bi, i: (bi, i, 0))
    headspec = lambda hh, w: pl.BlockSpec((1, hh, tm, w), lambda bi, i: (bi, 0, i, 0))
    full2 = lambda a: pl.BlockSpec(a.shape, lambda bi, i: (0, 0))
    full3 = lambda a: pl.BlockSpec(a.shape, lambda bi, i: (0, 0, 0))
    tabspec = pl.BlockSpec((tm, LANES), lambda bi, i: (i, 0))
    in_specs = [tokspec(d), full2(wts['w1']), full2(wts['wq']), full3(wts['wuk']), full2(wts['gq']), full2(wts['gkv'])]
    in_specs += [tabspec] * 6
    out_specs = (headspec(H_A, QC_W), tokspec(QC_W), tokspec(KV_RANK), tokspec(B_WIDTH), tokspec(B_WIDTH),
                 tokspec(LANES), tokspec(B_WIDTH), tokspec(B_WIDTH), tokspec(D_IDX), tokspec(B_WIDTH),
                 headspec(H_IDX, D_IDX), tokspec(C_WIDTH))
    return pl.pallas_call(
        _proj_body, out_shape=out_shape, grid=grid, in_specs=in_specs, out_specs=out_specs,
        compiler_params=_cparams(("parallel", "parallel")), name="proj",
    )(x, wts['w1'], wts['wq'], wts['wuk'], wts['gq'], wts['gkv'], *tabs)


def _flash_init(m_ref, l_ref, acc_ref):
    m_ref[...] = jnp.full(m_ref.shape, -jnp.inf, F32)
    l_ref[...] = jnp.zeros(l_ref.shape, F32)
    acc_ref[...] = jnp.zeros(acc_ref.shape, F32)


def _lane_tile(x, width):
    return x if width == LANES else jnp.concatenate([x] * (width // LANES), axis=1)


def _lane_fold(x):
    out = x[:, :LANES]
    for i in range(1, x.shape[1] // LANES):
        out = out + x[:, i * LANES:(i + 1) * LANES]
    return out


def _flash_step(q, k, v, keep, m_ref, l_ref, acc_ref, scale):
    s = _dot_nt(q, k) * (scale * LOG2E)
    if keep is not None:
        s = jnp.where(keep, s, -jnp.inf)
    _flash_update(s, v, m_ref, l_ref, acc_ref)


def _flash_update(s, v, m_ref, l_ref, acc_ref):
    m_prev = m_ref[...]
    m_new = jnp.maximum(m_prev, jnp.max(s, axis=-1, keepdims=True))
    m_use = jnp.where(m_new == -jnp.inf, 0.0, m_new)
    alpha = jnp.exp2(m_prev - m_use)
    p = jnp.exp2(s - _lane_tile(m_use, s.shape[1]))
    l_ref[...] = alpha * l_ref[...] + _lane_fold(p)
    acc_ref[...] = _lane_tile(alpha, v.shape[1]) * acc_ref[...] + _dot(p.astype(BF16), v)
    m_ref[...] = m_new


def _flash_result(l_ref, acc_ref):
    return acc_ref[...] * (1.0 / jnp.sum(l_ref[...], axis=-1, keepdims=True))


def _mla_body(qc_ref, kc_ref, o_ref, m_ref, l_ref, acc_ref, *, tq):
    qi = pl.program_id(1)
    q = qc_ref[0].reshape(H_A * tq, QC_W)
    _flash_init(m_ref, l_ref, acc_ref)

    def step(j, keep):
        k = kc_ref[0, pl.ds(pl.multiple_of(j * tq, tq), tq), :]
        _flash_step(q, k, k[:, :KV_RANK], keep, m_ref, l_ref, acc_ref, MLA_SCALE)

    def pair_step(p, carry):
        step(2 * p, None)
        step(2 * p + 1, None)
        return carry

    lax.fori_loop(0, qi // 2, pair_step, 0)

    @pl.when(qi % 2 == 1)
    def _():
        step(qi - 1, None)

    row = lax.broadcasted_iota(I32, (H_A * tq, tq), 0)
    col = lax.broadcasted_iota(I32, (H_A * tq, tq), 1)
    step(qi, col <= (row & (tq - 1)))
    o_ref[0] = _flash_result(l_ref, acc_ref).reshape(H_A, tq, KV_RANK).astype(BF16)


def _mla_call(qc, kc, tq):
    b, _, t, _ = qc.shape
    return pl.pallas_call(
        functools.partial(_mla_body, tq=tq),
        out_shape=jax.ShapeDtypeStruct((b, H_A, t, KV_RANK), BF16),
        grid=(b, t // tq),
        in_specs=[pl.BlockSpec((1, H_A, tq, QC_W), lambda bi, i: (bi, 0, i, 0)),
                  pl.BlockSpec((1, t, QC_W), lambda bi, i: (bi, 0, 0))],
        out_specs=pl.BlockSpec((1, H_A, tq, KV_RANK), lambda bi, i: (bi, 0, i, 0)),
        scratch_shapes=[pltpu.VMEM((H_A * tq, LANES), F32), pltpu.VMEM((H_A * tq, LANES), F32),
                        pltpu.VMEM((H_A * tq, KV_RANK), F32)],
        compiler_params=_cparams(("parallel", "parallel")), name="mla_prompt",
    )(qc, kc)


def _head_rows(q):
    lane = lax.broadcasted_iota(I32, q.shape, 1)
    return jnp.concatenate([jnp.where((lane >= h * DH_B) & (lane < (h + 1) * DH_B), q, 0.0) for h in range(H_B)],
                           axis=0)


def _head_lanes(o, r):
    lane = lax.broadcasted_iota(I32, (r, o.shape[1]), 1)
    out = jnp.zeros((r, o.shape[1]), F32)
    for h in range(H_B):
        out = jnp.where((lane >= h * DH_B) & (lane < (h + 1) * DH_B), o[h * r:(h + 1) * r], out)
    return out


def _dsa_body(qih_ref, aux_ref, kib_ref, qbb_ref, kbb_ref, vbb_ref, ob_ref, sc_ref, t_ref, cut_ref, m_ref, l_ref,
              acc_ref, *, tq, ck, topk, idx_bits):
    qi = pl.program_id(1)
    nk = (qi * tq) // ck + 1
    q4 = qih_ref[0].reshape(H_IDX * tq, D_IDX)
    w_t = aux_ref[0].T[AUX_WI0:AUX_WI0 + H_IDX, :] * (H_IDX ** -0.5)
    krow = lax.broadcasted_iota(I32, (ck, tq), 0)
    qpos = qi * tq + lax.broadcasted_iota(I32, (ck, tq), 1)

    def kpos_of(c):
        return c * ck + krow

    def score_chunk(c, carry):
        kc = kib_ref[0, pl.ds(pl.multiple_of(c * ck, ck), ck), :]
        d_t = _dot_nt(kc, q4) * (D_IDX ** -0.5)
        sc = jnp.zeros((ck, tq), F32)
        for h in range(H_IDX):
            sc = sc + jnp.maximum(d_t[:, h * tq:(h + 1) * tq], 0.0) * w_t[h:h + 1, :]
        sc_ref[c] = jnp.where(kpos_of(c) <= qpos, sc, -jnp.inf)
        return carry

    lax.fori_loop(0, nk, score_chunk, 0)

    kf = jnp.float32(topk)

    def search(n):
        def count(pred):
            part = jnp.zeros((8, tq), F32)
            for c in range(n):
                part = part + jnp.sum(jnp.where(pred(c), 1.0, 0.0).reshape(ck // 8, 8, tq), axis=0)
            return jnp.sum(part, axis=0, keepdims=True)

        def value_step(i, key):
            cand = key + (jnp.int32(1) << (jnp.int32(31) - i))
            cand_f = _key_to_float(cand)
            return jnp.where(count(lambda c: sc_ref[c] >= cand_f) >= kf, cand, key)

        t = _kth_to_float(lax.fori_loop(0, 32, value_step, jnp.full((1, tq), INT_MIN, I32)))
        need = kf - count(lambda c: sc_ref[c] > t)

        def tie(c):
            return (sc_ref[c] == t) & (kpos_of(c) <= qpos)

        t_ref[0:1, :] = t
        cut_ref[0:1, :] = jnp.full((1, tq), 2 ** idx_bits, I32)
        surplus = jnp.max(count(tie) - need)

        @pl.when(surplus > 0.0)
        def _():
            def index_step(i, cut):
                cand = cut + (jnp.int32(1) << (jnp.int32(idx_bits - 1) - i))
                return jnp.where(count(lambda c: tie(c) & (kpos_of(c) < cand)) < need, cand, cut)

            cut_ref[0:1, :] = lax.fori_loop(0, idx_bits, index_step, jnp.zeros((1, tq), I32))

    t_ref[0:1, :] = jnp.full((1, tq), -jnp.inf, F32)
    cut_ref[0:1, :] = jnp.zeros((1, tq), I32)
    for n in range(1, sc_ref.shape[0] + 1):
        pl.when((nk == n) & ((qi + 1) * tq > topk))(functools.partial(search, n))
    t = t_ref[0:1, :]
    cut = cut_ref[0:1, :]

    qm = _head_rows(qbb_ref[0].astype(F32)).astype(BF16)
    _flash_init(m_ref, l_ref, acc_ref)

    def attend(c, carry):
        sc = sc_ref[c]
        sel_t = (sc > t) | ((sc == t) & (kpos_of(c) <= qpos) & (kpos_of(c) <= cut))
        keep = jnp.where(sel_t, 1.0, 0.0).T
        keep = jnp.concatenate([keep] * H_B, axis=0) > 0.5
        rows = pl.ds(pl.multiple_of(c * ck, ck), ck)
        _flash_step(qm, kbb_ref[0, rows, :], vbb_ref[0, rows, :], keep, m_ref, l_ref, acc_ref, DH_B ** -0.5)
        return carry

    lax.fori_loop(0, nk, attend, 0)
    ob_ref[0] = _head_lanes(_flash_result(l_ref, acc_ref), tq).astype(BF16)


def _dsa_call(qih, aux, kib, qbb, kbb, vbb, tq, topk):
    b, t, _ = qbb.shape
    ck = _pick_tile(t, 256)
    idx_bits = max(1, int(np.ceil(np.log2(t))))
    whole = lambda w: pl.BlockSpec((1, t, w), lambda bi, i: (bi, 0, 0))
    tile = lambda w: pl.BlockSpec((1, tq, w), lambda bi, i: (bi, i, 0))
    return pl.pallas_call(
        functools.partial(_dsa_body, tq=tq, ck=ck, topk=topk, idx_bits=idx_bits),
        out_shape=jax.ShapeDtypeStruct((b, t, B_WIDTH), BF16),
        grid=(b, t // tq),
        in_specs=[pl.BlockSpec((1, H_IDX, tq, D_IDX), lambda bi, i: (bi, 0, i, 0)), tile(LANES), whole(D_IDX),
                  tile(B_WIDTH), whole(B_WIDTH), whole(B_WIDTH)],
        out_specs=tile(B_WIDTH),
        scratch_shapes=[pltpu.VMEM((t // ck, ck, tq), F32), pltpu.VMEM((8, tq), F32), pltpu.VMEM((8, tq), I32),
                        pltpu.VMEM((H_B * tq, LANES), F32),
                        pltpu.VMEM((H_B * tq, LANES), F32), pltpu.VMEM((H_B * tq, B_WIDTH), F32)],
        compiler_params=_cparams(("parallel", "parallel")), name="dsa_prompt",
    )(qih, aux, kib, qbb, kbb, vbb)


def _pool_delta(win_sums, cnts, xcur):
    lane = lax.broadcasted_iota(I32, xcur.shape, 1)
    mean = jnp.zeros(xcur.shape, F32)
    for g, w in enumerate(POOL_WINDOWS):
        in_group = (lane >= g * POOL_CH) & (lane < (g + 1) * POOL_CH)
        mean = jnp.where(in_group, win_sums[w] / cnts[w], mean)
    return mean - xcur


def _merge_tail(o_lat_heads, ob, delta, x, wuv_ref, woa_ref, wob_ref, woc_ref, wp_ref, cs_ref, g_ref, b_ref):
    oc = _dot(delta.astype(BF16), wp_ref[...]) * cs_ref[...]
    acc = _dot(ob, wob_ref[...]) + _dot(oc.astype(BF16), woc_ref[...])
    for h in range(H_A):
        oa = _dot(o_lat_heads(h), wuv_ref[h]).astype(BF16)
        acc = acc + _dot(oa, woa_ref[h])
    return _layer_norm(ALPHA * x + acc, g_ref[...], b_ref[...])


def _merge_body(ol_ref, ob_ref, xc_ref, buf_ref, x_ref, wuv_ref, woa_ref, wob_ref, woc_ref, wp_ref, cs_ref, g_ref,
                b_ref, h_ref, ext_ref, *, tm):
    i = pl.program_id(1)
    start = pl.multiple_of(i * tm, tm)

    @pl.when(i == 0)
    def _():
        ext_ref[0:HALO, :] = buf_ref[0]

    @pl.when(i > 0)
    def _():
        ext_ref[0:HALO, :] = xc_ref[0, pl.ds(start - HALO, HALO), :]

    xcur = xc_ref[0, pl.ds(start, tm), :]
    ext_ref[HALO:HALO + tm, :] = xcur
    run = xcur
    win_sums = {}
    for k in range(1, max(POOL_WINDOWS)):
        run = run + ext_ref[HALO - k:HALO - k + tm, :]
        if k + 1 in POOL_WINDOWS:
            win_sums[k + 1] = run
    pos1 = (start + 1 + lax.broadcasted_iota(I32, (tm, 1), 0)).astype(F32)
    cnts = {w: jnp.minimum(jnp.float32(w), pos1) for w in POOL_WINDOWS}
    delta = _pool_delta(win_sums, cnts, xcur)
    h_ref[0] = _merge_tail(lambda h: ol_ref[0, h], ob_ref[0], delta, x_ref[0], wuv_ref, woa_ref, wob_ref, woc_ref,
                           wp_ref, cs_ref, g_ref, b_ref)


def _merge_call(o_lat, ob, xc, buf, x, mw, tm):
    b, t, d = x.shape
    tile = lambda w: pl.BlockSpec((1, tm, w), lambda bi, i: (bi, i, 0))
    full = lambda a: pl.BlockSpec(a.shape, lambda bi, i: (0,) * a.ndim)
    wnames = ('wuv', 'woa', 'wob', 'woc', 'wp', 'cs', 'g1', 'b1')
    return pl.pallas_call(
        functools.partial(_merge_body, tm=tm),
        out_shape=jax.ShapeDtypeStruct((b, t, d), F32),
        grid=(b, t // tm),
        in_specs=[pl.BlockSpec((1, H_A, tm, KV_RANK), lambda bi, i: (bi, 0, i, 0)), tile(B_WIDTH),
                  pl.BlockSpec((1, t, C_WIDTH), lambda bi, i: (bi, 0, 0)),
                  pl.BlockSpec((1, HALO, C_WIDTH), lambda bi, i: (bi, 0, 0)), tile(d)]
                 + [full(mw[n]) for n in wnames],
        out_specs=tile(d),
        scratch_shapes=[pltpu.VMEM((HALO + tm, C_WIDTH), F32)],
        compiler_params=_cparams(("parallel", "arbitrary")), name="merge_prompt",
    )(o_lat, ob, xc, buf, x, *[mw[n] for n in wnames])


def _merge_sample_body(ol_ref, ob_ref, ext_ref, x_ref, wuv_ref, woa_ref, wob_ref, woc_ref, wp_ref, cs_ref, g_ref,
                       b_ref, h_ref, *, n_seen):
    n = ext_ref.shape[1]
    xcur = ext_ref[HALO - 1]
    run = xcur
    win_sums = {}
    for k in range(1, max(POOL_WINDOWS)):
        run = run + ext_ref[HALO - 1 - k]
        if k + 1 in POOL_WINDOWS:
            win_sums[k + 1] = run
    cnts = {w: jnp.full((n, 1), min(w, n_seen), F32) for w in POOL_WINDOWS}
    delta = _pool_delta(win_sums, cnts, xcur)
    h_ref[...] = _merge_tail(lambda h: ol_ref[h], ob_ref[...], delta, x_ref[...], wuv_ref, woa_ref, wob_ref, woc_ref,
                             wp_ref, cs_ref, g_ref, b_ref)


def _merge_sample_call(o_lat_h, ob, ext, x, mw, n_seen):
    n, d = x.shape
    wnames = ('wuv', 'woa', 'wob', 'woc', 'wp', 'cs', 'g1', 'b1')
    return pl.pallas_call(
        functools.partial(_merge_sample_body, n_seen=n_seen),
        out_shape=jax.ShapeDtypeStruct((n, d), F32), name="merge_sample",
        compiler_params=pltpu.CompilerParams(vmem_limit_bytes=VMEM_LIMIT),
    )(o_lat_h, ob, ext, x, *[mw[n_] for n_ in wnames])


def _swiglu_partial(xb, wg, wu, wd):
    a = _dot(xb, wg)
    u = _dot(xb, wu)
    hm = (a / (1.0 + jnp.exp(-a))) * u
    return _dot(hm.astype(BF16), wd)


def _ffn_body(h_ref, wg_ref, wu_ref, wd_ref, g_ref, b_ref, o_ref, xb_ref, acc_ref):
    k = pl.program_id(1)

    @pl.when(k == 0)
    def _():
        xb_ref[...] = h_ref[...].astype(BF16)
        acc_ref[...] = jnp.zeros(acc_ref.shape, F32)

    acc_ref[...] += _swiglu_partial(xb_ref[...], wg_ref[...], wu_ref[...], wd_ref[...])

    @pl.when(k == pl.num_programs(1) - 1)
    def _():
        o_ref[...] = _layer_norm(ALPHA * h_ref[...] + acc_ref[...], g_ref[...], b_ref[...])


def _ffn_call(h, wg, wu, wd, g, bta, tm, tf):
    n, d = h.shape
    f = wg.shape[1]
    return pl.pallas_call(
        _ffn_body, out_shape=jax.ShapeDtypeStruct((n, d), F32), grid=(n // tm, f // tf),
        in_specs=[pl.BlockSpec((tm, d), lambda i, k: (i, 0)), pl.BlockSpec((d, tf), lambda i, k: (0, k)),
                  pl.BlockSpec((d, tf), lambda i, k: (0, k)), pl.BlockSpec((tf, d), lambda i, k: (k, 0)),
                  pl.BlockSpec((1, d), lambda i, k: (0, 0)), pl.BlockSpec((1, d), lambda i, k: (0, 0))],
        out_specs=pl.BlockSpec((tm, d), lambda i, k: (i, 0)),
        scratch_shapes=[pltpu.VMEM((tm, d), BF16), pltpu.VMEM((tm, d), F32)],
        compiler_params=_cparams(("parallel", "arbitrary")), name="ffn_dense",
    )(h, wg, wu, wd, g, bta)


def _router_body(h_ref, r_ref, gate_ref):
    logits = jnp.dot(h_ref[...], r_ref[...], precision=lax.Precision.HIGHEST, preferred_element_type=F32)
    lane = lax.broadcasted_iota(I32, logits.shape, 1).astype(F32)
    logits = jnp.where(lane < N_EXPERTS, logits, -jnp.inf)
    v1 = jnp.max(logits, axis=-1, keepdims=True)
    i1 = jnp.min(jnp.where(logits == v1, lane, float(LANES)), axis=-1, keepdims=True)
    rest = jnp.where(lane == i1, -jnp.inf, logits)
    v2 = jnp.max(rest, axis=-1, keepdims=True)
    i2 = jnp.min(jnp.where(rest == v2, lane, float(LANES)), axis=-1, keepdims=True)
    e2 = jnp.exp(v2 - v1)
    g1 = 1.0 / (1.0 + e2)
    g2 = e2 / (1.0 + e2)
    dense = jnp.where(lane == i1, g1, jnp.where(lane == i2, g2, 0.0))
    for off, val in enumerate((i1, i2, g1, g2)):
        dense = jnp.where(lane == float(ROUTE_LANE0 + off), val, dense)
    gate_ref[...] = dense


def _router_call(h, router_pad, tm):
    n, d = h.shape
    return pl.pallas_call(
        _router_body, out_shape=jax.ShapeDtypeStruct((n, LANES), F32), grid=(n // tm,),
        in_specs=[pl.BlockSpec((tm, d), lambda i: (i, 0)), pl.BlockSpec((d, LANES), lambda i: (0, 0))],
        out_specs=pl.BlockSpec((tm, LANES), lambda i: (i, 0)),
        compiler_params=_cparams(("parallel",)), name="moe_router",
    )(h, router_pad)


def _moe_body(h_ref, gate_ref, wg_ref, wu_ref, wd_ref, g_ref, b_ref, o_ref, xb_ref, acc_ref):
    e = pl.program_id(1)
    k = pl.program_id(2)

    @pl.when((e == 0) & (k == 0))
    def _():
        xb_ref[...] = h_ref[...].astype(BF16)
        acc_ref[...] = jnp.zeros(acc_ref.shape, F32)

    gates = gate_ref[...]
    lane = lax.broadcasted_iota(I32, gates.shape, 1)
    ge = jnp.sum(jnp.where(lane == e, gates, 0.0), axis=-1, keepdims=True)
    acc_ref[...] += ge * _swiglu_partial(xb_ref[...], wg_ref[0], wu_ref[0], wd_ref[0])

    @pl.when((e == pl.num_programs(1) - 1) & (k == pl.num_programs(2) - 1))
    def _():
        o_ref[...] = _layer_norm(ALPHA * h_ref[...] + acc_ref[...], g_ref[...], b_ref[...])


def _moe_call(h, gates, wg, wu, wd, g, bta, tm, tf):
    n, d = h.shape
    ne, _, f = wg.shape
    return pl.pallas_call(
        _moe_body, out_shape=jax.ShapeDtypeStruct((n, d), F32), grid=(n // tm, ne, f // tf),
        in_specs=[pl.BlockSpec((tm, d), lambda i, e, k: (i, 0)), pl.BlockSpec((tm, LANES), lambda i, e, k: (i, 0)),
                  pl.BlockSpec((1, d, tf), lambda i, e, k: (e, 0, k)),
                  pl.BlockSpec((1, d, tf), lambda i, e, k: (e, 0, k)),
                  pl.BlockSpec((1, tf, d), lambda i, e, k: (e, k, 0)),
                  pl.BlockSpec((1, d), lambda i, e, k: (0, 0)), pl.BlockSpec((1, d), lambda i, e, k: (0, 0))],
        out_specs=pl.BlockSpec((tm, d), lambda i, e, k: (i, 0)),
        scratch_shapes=[pltpu.VMEM((tm, d), BF16), pltpu.VMEM((tm, d), F32)],
        compiler_params=_cparams(("parallel", "arbitrary", "arbitrary")), name="ffn_moe",
    )(h, gates, wg, wu, wd, g, bta)


def _route_plan(route, tm):
    n = route.shape[0]
    ea = route[:, ROUTE_LANE0:ROUTE_LANE0 + 2].astype(I32).T.reshape(-1)
    ga = route[:, ROUTE_LANE0 + 2:ROUTE_LANE0 + 4].T.reshape(-1)
    onehot = (ea[:, None] == jnp.arange(N_EXPERTS, dtype=I32)[None, :]).astype(I32)
    csum = jnp.cumsum(onehot, axis=0)
    counts = csum[-1]
    rank = jnp.sum(onehot * csum, axis=1) - 1
    tiles_e = (counts + tm - 1) // tm
    tile_end = jnp.cumsum(tiles_e)
    tile_off = tile_end - tiles_e
    pos = tile_off[ea] * tm + rank
    n_tiles = (2 * n) // tm + N_EXPERTS
    pair = jnp.arange(2 * n, dtype=I32)
    pair_of = jnp.full((n_tiles * tm,), -1, I32).at[pos].set(pair)
    real = pair_of >= 0
    tok_of = jnp.where(real, pair_of % n, 0)
    spare = 2 * n + jnp.arange(n_tiles * tm, dtype=I32) % tm
    dst_of = jnp.where(real, pair_of, spare)
    gate_of = jnp.where(real, ga[jnp.maximum(pair_of, 0)], 0.0)
    tile_ids = jnp.arange(n_tiles, dtype=I32)
    te = jnp.minimum(jnp.sum((tile_ids[:, None] >= tile_end[None, :]).astype(I32), axis=1), N_EXPERTS - 1)
    nv = jnp.clip(counts[te] - (tile_ids - tile_off[te]) * tm, 0, tm)
    return (tok_of.reshape(n_tiles, 1, tm), dst_of.reshape(n_tiles, 1, tm), gate_of.reshape(n_tiles * tm, 1),
            te.astype(I32), nv.astype(I32))


def _moe_routed_body(te_ref, nv_ref, tokc_ref, tokn_ref, dst_ref, gate_ref, h_hbm, wg_ref, wu_ref, wd_ref, y_hbm,
                     xbuf, xb_ref, acc_ref, ybuf, gsem, ssem, *, tm):
    i = pl.program_id(0)
    k = pl.program_id(1)
    n_t = pl.num_programs(0)
    slot = i % 2

    def gather_copy(tok_ref, r, buf_slot):
        return pltpu.make_async_copy(h_hbm.at[pl.ds(tok_ref[0, 0, r], 1), :], xbuf.at[buf_slot, pl.ds(r, 1), :],
                                     gsem.at[buf_slot])

    def scatter_copy(r):
        return pltpu.make_async_copy(ybuf.at[pl.ds(r, 1), :], y_hbm.at[pl.ds(dst_ref[0, 0, r], 1), :], ssem.at[0])

    def start_rows(copy_of):
        def body(r, carry):
            copy_of(r).start()
            return carry
        lax.fori_loop(0, tm, body, 0, unroll=8)

    def wait_gather(buf_slot):
        pltpu.make_async_copy(h_hbm.at[pl.ds(0, tm), :], xbuf.at[buf_slot], gsem.at[buf_slot]).wait()

    def wait_scatter():
        pltpu.make_async_copy(ybuf, y_hbm.at[pl.ds(0, tm), :], ssem.at[0]).wait()

    @pl.when(k == 0)
    def _():
        @pl.when(i == 0)
        def _():
            start_rows(lambda r: gather_copy(tokc_ref, r, 0))

        wait_gather(slot)

        @pl.when(i + 1 < n_t)
        def _():
            start_rows(lambda r: gather_copy(tokn_ref, r, 1 - slot))

        xb_ref[...] = xbuf[slot].astype(BF16)
        acc_ref[...] = jnp.zeros(acc_ref.shape, F32)

    @pl.when(nv_ref[i] > 0)
    def _():
        acc_ref[...] += _swiglu_partial(xb_ref[...], wg_ref[0], wu_ref[0], wd_ref[0])

    @pl.when(k == pl.num_programs(1) - 1)
    def _():
        @pl.when(i > 0)
        def _():
            wait_scatter()

        ybuf[...] = acc_ref[...] * gate_ref[...]
        start_rows(scatter_copy)

        @pl.when(i == n_t - 1)
        def _():
            wait_scatter()


def _moe_routed_call(h, route, wg, wu, wd, tm, tf):
    n, d = h.shape
    f = wg.shape[2]
    tok_of, dst_of, gate_of, te, nv = _route_plan(route, tm)
    n_tiles = tok_of.shape[0]
    smem_row = lambda fn: pl.BlockSpec((1, 1, tm), fn, memory_space=pltpu.SMEM)
    grid_spec = pltpu.PrefetchScalarGridSpec(
        num_scalar_prefetch=2, grid=(n_tiles, f // tf),
        in_specs=[smem_row(lambda i, k, te_, nv_: (i, 0, 0)),
                  smem_row(lambda i, k, te_, nv_: (jnp.minimum(i + 1, n_tiles - 1), 0, 0)),
                  smem_row(lambda i, k, te_, nv_: (i, 0, 0)),
                  pl.BlockSpec((tm, 1), lambda i, k, te_, nv_: (i, 0)),
                  pl.BlockSpec(memory_space=pl.ANY),
                  pl.BlockSpec((1, d, tf), lambda i, k, te_, nv_: (te_[i], 0, k)),
                  pl.BlockSpec((1, d, tf), lambda i, k, te_, nv_: (te_[i], 0, k)),
                  pl.BlockSpec((1, tf, d), lambda i, k, te_, nv_: (te_[i], k, 0))],
        out_specs=pl.BlockSpec(memory_space=pl.ANY),
        scratch_shapes=[pltpu.VMEM((2, tm, d), F32), pltpu.VMEM((tm, d), BF16), pltpu.VMEM((tm, d), F32),
                        pltpu.VMEM((tm, d), F32), pltpu.SemaphoreType.DMA((2,)), pltpu.SemaphoreType.DMA((1,))])
    return pl.pallas_call(
        functools.partial(_moe_routed_body, tm=tm),
        out_shape=jax.ShapeDtypeStruct((2 * n + tm, d), F32), grid_spec=grid_spec,
        compiler_params=_cparams(("arbitrary", "arbitrary")), name="ffn_moe_routed",
    )(te, nv, tok_of, tok_of, dst_of, gate_of, h, wg, wu, wd)


def _combine_body(h_ref, y0_ref, y1_ref, g_ref, b_ref, o_ref):
    o_ref[...] = _layer_norm(ALPHA * h_ref[...] + (y0_ref[...] + y1_ref[...]), g_ref[...], b_ref[...])


def _combine_call(h, y2, g, bta, tm):
    n, d = h.shape
    return pl.pallas_call(
        _combine_body, out_shape=jax.ShapeDtypeStruct((n, d), F32), grid=(n // tm,),
        in_specs=[pl.BlockSpec((tm, d), lambda i: (i, 0)), pl.BlockSpec((tm, d), lambda i: (i, 0)),
                  pl.BlockSpec((tm, d), lambda i: (i + n // tm, 0)),
                  pl.BlockSpec((1, d), lambda i: (0, 0)), pl.BlockSpec((1, d), lambda i: (0, 0))],
        out_specs=pl.BlockSpec((tm, d), lambda i: (i, 0)),
        compiler_params=_cparams(("parallel",)), name="moe_combine",
    )(h, y2, y2, g, bta)


def _stream_page_groups(pt_ref, layer, group, streams, sems, compute):
    b = pl.program_id(0)
    n_seq = pl.num_programs(0)
    n_groups = pt_ref.shape[1] // group

    def start(seq, g, slot):
        for j in range(group):
            page = pt_ref[seq, g * group + j]
            for a, (hbm, buf) in enumerate(streams):
                pltpu.make_async_copy(hbm.at[layer, page], buf.at[slot, j], sems.at[a, slot]).start()

    def wait(slot):
        for a, (hbm, buf) in enumerate(streams):
            pltpu.make_async_copy(hbm.at[layer, pl.ds(0, group)], buf.at[slot], sems.at[a, slot]).wait()

    @pl.when(b == 0)
    def _():
        start(0, 0, 0)

    def body(g, carry):
        slot = g % 2
        wait(slot)

        @pl.when(g + 1 < n_groups)
        def _():
            start(b, g + 1, 1 - slot)

        @pl.when((g + 1 == n_groups) & (b + 1 < n_seq))
        def _():
            start(b + 1, 0, 1 - slot)

        compute(g, slot)
        return carry

    lax.fori_loop(0, n_groups, body, 0)


def _mla_idx_sample_body(pt_ref, q_ref, knew_ref, qi_ref, w_ref, ck_hbm, kr_hbm, ki_hbm, o_ref, sc_ref,
                         ckbuf, krbuf, kibuf, sems, m_ref, l_ref, acc_ref, *, layer, group):
    _flash_init(m_ref, l_ref, acc_ref)
    q = q_ref[0]
    q_rope = q[:, KV_RANK + ROPE_LANE0:KV_RANK + ROPE_LANE0 + ROPE_A]
    w = w_ref[0] * (H_IDX ** -0.5)

    def compute(g, slot):
        ck = ckbuf[slot].reshape(group * PAGE_SIZE, KV_RANK).astype(BF16)
        kr_t = jnp.concatenate([krbuf[slot, j] for j in range(group)], axis=1).astype(BF16)
        s = (_dot_nt(q[:, :KV_RANK], ck) + _dot(q_rope, kr_t)) * (MLA_SCALE * LOG2E)
        _flash_update(s, ck, m_ref, l_ref, acc_ref)
        ki_t = jnp.concatenate([kibuf[slot, j] for j in range(group)], axis=1).astype(BF16)
        dots = _dot(qi_ref[0], ki_t) * (D_IDX ** -0.5)
        sc_ref[0, pl.ds(g, 1), :] = jnp.sum(jnp.maximum(dots, 0.0) * w, axis=0, keepdims=True)

    _stream_page_groups(pt_ref, layer, group, ((ck_hbm, ckbuf), (kr_hbm, krbuf), (ki_hbm, kibuf)), sems, compute)
    kn = knew_ref[0].astype(F32)
    s_n = jnp.sum(q.astype(F32) * kn, axis=-1, keepdims=True) * (MLA_SCALE * LOG2E)
    m_old = m_ref[...]
    m_fin = jnp.maximum(m_old, s_n)
    a = jnp.exp2(m_old - m_fin)
    p_n = jnp.exp2(s_n - m_fin)[:, 0:1]
    l_fin = jnp.sum(a * l_ref[...], axis=-1, keepdims=True) + p_n
    acc = _lane_tile(a, KV_RANK) * acc_ref[...] + p_n * kn[:, :KV_RANK]
    o_ref[0] = (acc * (1.0 / l_fin)).astype(BF16)


def _page_group(n_pages):
    g = _pick_tile(n_pages, 8)
    return g if (n_pages // g) % 2 == 0 else max(1, g // 2)


def _mla_idx_sample_call(page_table, q, knew, qi, wi, cache_ckv, cache_kr_t, cache_ki_t, layer):
    n, n_pages = page_table.shape
    group = _page_group(n_pages)
    n_groups = n_pages // group
    assert n_groups % 2 == 0
    per_seq = lambda *blk: pl.BlockSpec((1,) + blk, lambda bi, pt: (bi,) + (0,) * len(blk))
    any_spec = pl.BlockSpec(memory_space=pl.ANY)
    grid_spec = pltpu.PrefetchScalarGridSpec(
        num_scalar_prefetch=1, grid=(n,),
        in_specs=[per_seq(H_A, QC_W), per_seq(1, QC_W), per_seq(H_IDX, D_IDX), per_seq(H_IDX, 1),
                  any_spec, any_spec, any_spec],
        out_specs=(per_seq(H_A, KV_RANK), per_seq(n_groups, group * PAGE_SIZE)),
        scratch_shapes=[pltpu.VMEM((2, group, PAGE_SIZE, KV_RANK), F32), pltpu.VMEM((2, group, ROPE_A, PAGE_SIZE), F32),
                        pltpu.VMEM((2, group, D_IDX, PAGE_SIZE), F32), pltpu.SemaphoreType.DMA((3, 2)),
                        pltpu.VMEM((H_A, LANES), F32), pltpu.VMEM((H_A, LANES), F32),
                        pltpu.VMEM((H_A, KV_RANK), F32)])
    o_lat, score = pl.pallas_call(
        functools.partial(_mla_idx_sample_body, layer=layer, group=group),
        out_shape=(jax.ShapeDtypeStruct((n, H_A, KV_RANK), BF16),
                   jax.ShapeDtypeStruct((n, n_groups, group * PAGE_SIZE), F32)),
        grid_spec=grid_spec, compiler_params=_cparams(("arbitrary",)), name="mla_idx_sample",
    )(page_table, q, knew, qi, wi, cache_ckv, cache_kr_t, cache_ki_t)
    return o_lat, score.reshape(n, n_pages * PAGE_SIZE)


def _select_sample_body(sp_ref, qi_ref, w_ref, kin_ref, sel_ref, *, past, topk, idx_bits):
    n = sp_ref.shape[0]
    kn = kin_ref[...].astype(F32)
    s_new = jnp.zeros((n, 1), F32)
    for h in range(H_IDX):
        d = jnp.sum(qi_ref[h].astype(F32) * kn, axis=-1, keepdims=True) * (D_IDX ** -0.5)
        s_new = s_new + jnp.maximum(d, 0.0) * (w_ref[:, h:h + 1] * (H_IDX ** -0.5))
    lane = lax.broadcasted_iota(I32, (n, LANES), 1)
    tail = jnp.where(lane == 0, s_new, 0.0)
    score = jnp.concatenate([sp_ref[...], tail], axis=-1)
    kpos = lax.broadcasted_iota(I32, score.shape, 1)
    sel_ref[...] = jnp.where(_topk_mask(score, kpos <= past, kpos, topk, idx_bits), 1.0, 0.0)


def _select_sample_call(score_past, qi, wi, ki_new, topk):
    n, past = score_past.shape
    idx_bits = int(np.ceil(np.log2(past + LANES)))
    return pl.pallas_call(
        functools.partial(_select_sample_body, past=past, topk=topk, idx_bits=idx_bits),
        out_shape=jax.ShapeDtypeStruct((n, past + LANES), F32),
        compiler_params=pltpu.CompilerParams(vmem_limit_bytes=VMEM_LIMIT), name="select_sample",
    )(score_past, qi, wi, ki_new)


def _dsa_sample_body(pt_ref, q_ref, sel_ref, seln_ref, kn_ref, vn_ref, k_hbm, v_hbm, o_ref, kbuf, vbuf, sems,
                     m_ref, l_ref, acc_ref, *, layer, group):
    _flash_init(m_ref, l_ref, acc_ref)
    qcol = q_ref[0]
    scale = (DH_B ** -0.5) * LOG2E

    def compute(g, slot):
        keep = sel_ref[0, pl.ds(g, 1), :] > 0.5
        for h in range(H_B):
            qh = qcol[h * DH_B:(h + 1) * DH_B]
            k_t = jnp.concatenate([kbuf[slot, j, h] for j in range(group)], axis=1)
            s = jnp.sum(k_t * qh, axis=0, keepdims=True) * scale
            s = jnp.where(keep, s, -jnp.inf)
            m_prev = m_ref[h]
            m_new = jnp.maximum(m_prev, jnp.max(s, axis=-1, keepdims=True))
            m_use = jnp.where(m_new == -jnp.inf, 0.0, m_new)
            alpha = jnp.exp2(m_prev - m_use)
            p = jnp.exp2(s - _lane_tile(m_use, s.shape[1]))
            v_t = jnp.concatenate([vbuf[slot, j, h] for j in range(group)], axis=1)
            l_ref[h] = alpha * l_ref[h] + _lane_fold(p)
            acc_ref[h] = alpha * acc_ref[h] + _lane_fold(v_t * p)
            m_ref[h] = m_new

    _stream_page_groups(pt_ref, layer, group, ((k_hbm, kbuf), (v_hbm, vbuf)), sems, compute)
    new_kept = seln_ref[0][:, 0:1] > 0.5
    for h in range(H_B):
        rows = slice(h * DH_B, (h + 1) * DH_B)
        s_n = jnp.sum(kn_ref[0][rows] * qcol[rows], axis=0, keepdims=True) * scale
        s_n = jnp.where(new_kept, s_n, -jnp.inf)
        m_prev = m_ref[h]
        m_new = jnp.maximum(m_prev, s_n)
        m_use = jnp.where(m_new == -jnp.inf, 0.0, m_new)
        alpha = jnp.exp2(m_prev - m_use)
        p_n = jnp.exp2(s_n - m_use)[:, 0:1]
        l_tot = jnp.sum(alpha * l_ref[h], axis=-1, keepdims=True) + p_n
        acc = jnp.sum(alpha * acc_ref[h], axis=-1, keepdims=True) + p_n * vn_ref[0][rows]
        o_ref[0, rows, :] = acc * (1.0 / l_tot)


def _dsa_sample_call(page_table, q_col, sel, kn_col, vn_col, cache_k_t, cache_v_t, layer):
    n, n_pages = page_table.shape
    group = _page_group(n_pages)
    n_groups = n_pages // group
    assert n_groups % 2 == 0
    past = n_pages * PAGE_SIZE
    sel_past = sel[:, :past].reshape(n, n_groups, group * PAGE_SIZE)
    sel_new = sel[:, past:].reshape(n, 1, LANES)
    per_seq = lambda *blk: pl.BlockSpec((1,) + blk, lambda bi, pt: (bi,) + (0,) * len(blk))
    any_spec = pl.BlockSpec(memory_space=pl.ANY)
    grid_spec = pltpu.PrefetchScalarGridSpec(
        num_scalar_prefetch=1, grid=(n,),
        in_specs=[per_seq(B_WIDTH, 1), per_seq(n_groups, group * PAGE_SIZE), per_seq(1, LANES), per_seq(B_WIDTH, 1),
                  per_seq(B_WIDTH, 1), any_spec, any_spec],
        out_specs=per_seq(B_WIDTH, 1),
        scratch_shapes=[pltpu.VMEM((2, group, H_B, DH_B, PAGE_SIZE), F32),
                        pltpu.VMEM((2, group, H_B, DH_B, PAGE_SIZE), F32), pltpu.SemaphoreType.DMA((2, 2)),
                        pltpu.VMEM((H_B, 1, LANES), F32), pltpu.VMEM((H_B, 1, LANES), F32),
                        pltpu.VMEM((H_B, DH_B, LANES), F32)])
    return pl.pallas_call(
        functools.partial(_dsa_sample_body, layer=layer, group=group),
        out_shape=jax.ShapeDtypeStruct((n, B_WIDTH, 1), F32), grid_spec=grid_spec,
        compiler_params=_cparams(("arbitrary",)), name="dsa_sample",
    )(page_table, q_col, sel_past, sel_new, kn_col, vn_col, cache_k_t, cache_v_t)


def _rot_cols(w, head_dim):
    k, n = w.shape
    wh = w.reshape(k, n // head_dim, 2, head_dim // 2)
    return jnp.concatenate([-wh[:, :, 1], wh[:, :, 0]], axis=-1).reshape(k, n)


def _prep_layer(w_in, a_q_norm, a_kv_norm, a_w_uq, a_w_uk, a_w_uv, c_w_pool, c_scale, w_out, ln_g, ln_b):
    offs = np.cumsum((0,) + IN_SPLITS)
    cq, ckv, kr, qb, kb, vb, qi, ki, wi, xc = [w_in[:, offs[i]:offs[i + 1]] for i in range(len(IN_SPLITS))]
    d = w_in.shape[0]
    zeros = lambda n: jnp.zeros((d, n), F32)
    grp_a = jnp.concatenate([ki, kr, wi, zeros(LANES - D_IDX - ROPE_A - H_IDX)], axis=1)
    grp_ar = jnp.concatenate([_rot_cols(ki, D_IDX), _rot_cols(kr, ROPE_A), zeros(LANES - D_IDX - ROPE_A)], axis=1)
    w1 = jnp.concatenate([cq, ckv, qb, _rot_cols(qb, DH_B), kb, _rot_cols(kb, DH_B), vb, qi, _rot_cols(qi, D_IDX),
                          xc, grp_a, grp_ar], axis=1).astype(BF16)
    uq = a_w_uq.reshape(Q_RANK, H_A, NOPE_A + ROPE_A)
    nope = jnp.pad(uq[:, :, :NOPE_A], ((0, 0), (0, 0), (0, LANES - NOPE_A)))
    rope = uq[:, :, NOPE_A:]
    rope_rot = jnp.concatenate([-rope[..., ROPE_A // 2:], rope[..., :ROPE_A // 2]], axis=-1)
    pad_rope = lambda r: jnp.pad(r, ((0, 0), (0, 0), (ROPE_LANE0, LANES - ROPE_LANE0 - ROPE_A)))
    wq = jnp.concatenate([nope.reshape(Q_RANK, -1), pad_rope(rope).reshape(Q_RANK, -1),
                          pad_rope(rope_rot).reshape(Q_RANK, -1)], axis=1).astype(BF16)
    wuk = jnp.pad(jnp.transpose(a_w_uk, (1, 2, 0)), ((0, 0), (0, LANES - NOPE_A), (0, 0))).astype(BF16)
    proj_w = dict(w1=w1, wq=wq, wuk=wuk, gq=a_q_norm.reshape(1, -1), gkv=a_kv_norm.reshape(1, -1))
    wuv = jnp.pad(jnp.transpose(a_w_uv, (1, 0, 2)), ((0, 0), (0, 0), (0, LANES - V_A))).astype(BF16)
    woa = jnp.pad(w_out[:A_WIDTH].reshape(H_A, V_A, -1), ((0, 0), (0, LANES - V_A), (0, 0))).astype(BF16)
    wp = jnp.zeros((C_WIDTH, C_WIDTH), F32)
    for g in range(len(POOL_WINDOWS)):
        wp = wp.at[g * POOL_CH:(g + 1) * POOL_CH, g * POOL_CH:(g + 1) * POOL_CH].set(c_w_pool[g])
    merge_w = dict(wuv=wuv, woa=woa, wob=w_out[A_WIDTH:A_WIDTH + B_WIDTH].astype(BF16),
                   woc=w_out[A_WIDTH + B_WIDTH:].astype(BF16), wp=wp.astype(BF16), cs=c_scale.reshape(1, -1),
                   g1=ln_g.reshape(1, -1), b1=ln_b.reshape(1, -1))
    return proj_w, merge_w


def _rope_tables(pos):
    posf = pos.astype(F32)[:, None]

    def cs(dim):
        inv = ROPE_THETA ** (-jnp.arange(0, dim, 2, dtype=F32) / dim)
        ang = posf * inv[None, :]
        c, s = jnp.cos(ang), jnp.sin(ang)
        return jnp.concatenate([c, c], -1), jnp.concatenate([s, s], -1)

    c64, s64 = cs(DH_B)
    c32, s32 = cs(ROPE_A)
    t = pos.shape[0]
    z = lambda n: jnp.zeros((t, n), F32)
    c_pair, s_pair = jnp.concatenate([c64, c64], -1), jnp.concatenate([s64, s64], -1)
    c_a = jnp.concatenate([c64, c32, jnp.ones((t, H_IDX), F32), z(LANES - D_IDX - ROPE_A - H_IDX)], -1)
    s_a = jnp.concatenate([s64, s32, z(LANES - D_IDX - ROPE_A)], -1)
    c_q = jnp.concatenate([z(ROPE_LANE0), c32, z(LANES - ROPE_LANE0 - ROPE_A)], -1)
    s_q = jnp.concatenate([z(ROPE_LANE0), s32, z(LANES - ROPE_LANE0 - ROPE_A)], -1)
    return c_pair, s_pair, c_a, s_a, c_q, s_q


def _aux_split(aux):
    return aux[..., D_IDX:D_IDX + ROPE_A], aux[..., :D_IDX]


def _mixers_prompt(x, proj_w, merge_w, tabs, tiles):
    b, t, _ = x.shape
    (qc, kc, ckv_n, kb, vb, aux, kbb, vbb, kib, qbb, qih, xc) = _proj_call(x, proj_w, tabs, tiles['proj'])
    o_lat = _mla_call(qc, kc, tiles['mla'])
    ob = _dsa_call(qih, aux, kib, qbb, kbb, vbb, tiles['dsa'], min(TOPK_MAX, t // 4))
    buf = jnp.zeros((b, HALO, C_WIDTH), F32)
    h = _merge_call(o_lat, ob, xc, buf, x, merge_w, tiles['merge'])
    kr_r, ki_r = _aux_split(aux)
    state = (ckv_n, kr_r, kb.reshape(b, t, H_B, DH_B), vb.reshape(b, t, H_B, DH_B), ki_r, xc[:, t - POOL_BUF:])
    return h, state


def _mixers_sample(x, proj_w, merge_w, tabs, caches, state_pool, page_table, layer):
    n = x.shape[0]
    c_ckv, c_kr, c_k, c_v, c_kidx = caches
    (qc, kc, ckv_n, kb, vb, aux, kbb, vbb, kib, qbb, qih, xc) = _proj_call(x.reshape(1, n, -1), proj_w, tabs, n)
    q = jnp.transpose(qc[0], (1, 0, 2))
    qi = jnp.transpose(qih[0], (1, 0, 2))
    wi = aux[0][:, AUX_WI0:AUX_WI0 + H_IDX]
    o_lat, score_past = _mla_idx_sample_call(page_table, q, kc[0][:, None, :], qi, wi[:, :, None], c_ckv, c_kr,
                                             c_kidx, layer)
    past = score_past.shape[1]
    sel = _select_sample_call(score_past, qih[0], wi, kib[0], min(TOPK_MAX, (past + 1) // 4))
    col = lambda a: a.astype(F32)[:, :, None]
    ob = _dsa_sample_call(page_table, col(qbb[0]), sel, col(kb[0]), col(vb[0]), c_k, c_v,
                          layer)[:, :, 0].astype(BF16)
    ext = jnp.concatenate([state_pool, xc[0][:, None, :]], axis=1)
    h = _merge_sample_call(jnp.transpose(o_lat, (1, 0, 2)), ob, jnp.transpose(ext, (1, 0, 2)), x, merge_w, past + 1)
    kr_r, ki_r = _aux_split(aux[0])
    state = (ckv_n[0][:, None], kr_r[:, None], kb[0].reshape(n, 1, H_B, DH_B), vb[0].reshape(n, 1, H_B, DH_B),
             ki_r[:, None], ext[:, HALO - POOL_BUF:])
    return h, state


def _pick_tile(n, pref):
    t = min(n, pref)
    while n % t:
        t //= 2
    return t


def kernel(x_prompt, x_sample, cache_a_ckv, cache_a_krope, cache_b_k, cache_b_v, cache_b_kidx, state_pool, page_table, w_in, a_q_norm, a_kv_norm, a_w_uq, a_w_uk, a_w_uv, c_w_pool, c_scale, w_out, ln1_g, ln1_b, ln2_g, ln2_b, ffn_w_gate, ffn_w_up, ffn_w_down, moe_router, moe_w_gate, moe_w_up, moe_w_down):
    bsz, seq, d = x_prompt.shape
    n_dec = x_sample.shape[0]
    depth = w_in.shape[0]
    n_pages = page_table.shape[1]
    past = n_pages * PAGE_SIZE
    caches = (cache_a_ckv, jnp.transpose(cache_a_krope, (0, 1, 3, 2)), jnp.transpose(cache_b_k, (0, 1, 3, 4, 2)),
              jnp.transpose(cache_b_v, (0, 1, 3, 4, 2)), jnp.transpose(cache_b_kidx, (0, 1, 3, 2)))
    tiles = dict(proj=_pick_tile(seq, 512), mla=_pick_tile(seq, 256), dsa=_pick_tile(seq, 128),
                 merge=_pick_tile(seq, 512))
    tabs_p = _rope_tables(jnp.arange(seq, dtype=I32))
    tabs_s = _rope_tables(jnp.full((n_dec,), past, I32))
    hp = x_prompt
    hs = x_sample.reshape(n_dec, d)
    new_p, new_s = [], []
    for l in range(depth):
        proj_w, merge_w = _prep_layer(w_in[l], a_q_norm[l], a_kv_norm[l], a_w_uq[l], a_w_uk[l], a_w_uv[l],
                                      c_w_pool[l], c_scale[l], w_out[l], ln1_g[l], ln1_b[l])
        hp, st_p = _mixers_prompt(hp, proj_w, merge_w, tabs_p, tiles)
        hs, st_s = _mixers_sample(hs, proj_w, merge_w, tabs_s, caches, state_pool[l], page_table, l)
        g2, b2 = ln2_g[l].reshape(1, -1), ln2_b[l].reshape(1, -1)
        hp2 = hp.reshape(bsz * seq, d)
        j = l // 2
        tm = _pick_tile(bsz * seq, 1024)
        if l % 2 == 0:
            wg, wu, wd = ffn_w_gate[j].astype(BF16), ffn_w_up[j].astype(BF16), ffn_w_down[j].astype(BF16)
            hp2 = _ffn_call(hp2, wg, wu, wd, g2, b2, tm, 512)
            hs = _ffn_call(hs, wg, wu, wd, g2, b2, n_dec, 512)
        else:
            wg, wu, wd = moe_w_gate[j].astype(BF16), moe_w_up[j].astype(BF16), moe_w_down[j].astype(BF16)
            router = jnp.pad(moe_router[j], ((0, 0), (0, LANES - N_EXPERTS)))
            y2 = _moe_routed_call(hp2, _router_call(hp2, router, tm), wg, wu, wd, _pick_tile(bsz * seq, MOE_ROW_TILE), 512)
            hp2 = _combine_call(hp2, y2, g2, b2, tm)
            hs = _moe_call(hs, _router_call(hs, router, n_dec), wg, wu, wd, g2, b2, n_dec, 512)
        hp = hp2.reshape(bsz, seq, d)
        new_p.append(st_p)
        new_s.append(st_s)
    outs_p = [jnp.stack(a) for a in zip(*new_p)]
    outs_s = [jnp.stack(a) for a in zip(*new_s)]
    return (hp, hs.reshape(n_dec, 1, d), *outs_p, *outs_s)
```

```python
import functools

import numpy as np
import jax
import jax.numpy as jnp
from jax import lax
from jax.experimental import pallas as pl
from jax.experimental.pallas import tpu as pltpu

F32 = jnp.float32
BF16 = jnp.bfloat16
I32 = jnp.int32

D_MODEL = 1024
DEPTH = 2
PAGE_SIZE = 128
H_A, NOPE_A, ROPE_A, V_A = 8, 64, 32, 64
Q_RANK, KV_RANK = 256, 256
MLA_SCALE = (NOPE_A + ROPE_A) ** -0.5
H_B, DH_B = 4, 64
H_IDX, D_IDX = 4, 64
TOPK_MAX = 256
POOL_WINDOWS = (2, 4, 8, 16)
POOL_CH = 64
C_WIDTH = len(POOL_WINDOWS) * POOL_CH
POOL_BUF = max(POOL_WINDOWS) - 1
A_WIDTH = H_A * V_A
B_WIDTH = H_B * DH_B
IN_SPLITS = (Q_RANK, KV_RANK, ROPE_A, B_WIDTH, B_WIDTH, B_WIDTH, H_IDX * D_IDX, D_IDX, H_IDX, C_WIDTH)
D_FF = 3584
N_EXPERTS = 8
ROPE_THETA = 10000.0
ALPHA = (2 * DEPTH) ** 0.25
LN_EPS = 1e-5
RMS_EPS = 1e-6

LANES = 128
HALO = 16
QC_W = KV_RANK + LANES
ROPE_LANE0 = 64
AUX_WI0 = 96
INT_MIN = -(2 ** 31)
LOG2E = 1.4426950408889634
ROUTE_LANE0 = N_EXPERTS
MOE_ROW_TILE = 512
MOE_FF_TILE = 896
STREAM_SLOTS = 4
VMEM_LIMIT = 56 * 1024 * 1024


def _cparams(sem):
    return pltpu.CompilerParams(dimension_semantics=sem, vmem_limit_bytes=VMEM_LIMIT)


def _rms(x, g):
    return x * lax.rsqrt(jnp.mean(x * x, -1, keepdims=True) + RMS_EPS) * g


def _layer_norm(x, g, b):
    mu = jnp.mean(x, -1, keepdims=True)
    xc = x - mu
    var = jnp.mean(xc * xc, -1, keepdims=True)
    return xc * lax.rsqrt(var + LN_EPS) * g + b


def _dot(a, b):
    return jnp.dot(a, b, preferred_element_type=F32)


def _dot_nt(a, b):
    return lax.dot_general(a, b, (((1,), (1,)), ((), ())), preferred_element_type=F32)


def _count(mask):
    return jnp.sum(jnp.where(mask, 1.0, 0.0), axis=-1, keepdims=True)


def _key_to_float(key):
    return lax.bitcast_convert_type(key ^ ((key >> 31) & jnp.int32(0x7FFFFFFF)), F32)


def _kth_to_float(key):
    return jnp.where(key == jnp.int32(INT_MIN), -jnp.inf, _key_to_float(key))


def _topk_mask(score, valid, kpos, k, idx_bits):
    score = jnp.where(valid, score, -jnp.inf)
    rows = score.shape[0]
    kf = jnp.float32(k)

    def value_step(i, t):
        cand = t + (jnp.int32(1) << (jnp.int32(31) - i))
        return jnp.where(_count(score >= _key_to_float(cand)) >= kf, cand, t)

    t = _kth_to_float(lax.fori_loop(0, 32, value_step, jnp.full((rows, 1), INT_MIN, I32)))
    gt = score > t
    eq = (score == t) & valid
    need = kf - _count(gt)

    def index_step(i, c):
        cand = c + (jnp.int32(1) << (jnp.int32(idx_bits - 1) - i))
        return jnp.where(_count(eq & (kpos < cand)) < need, cand, c)

    c = lax.fori_loop(0, idx_bits, index_step, jnp.zeros((rows, 1), I32))
    return valid & (gt | (eq & (kpos <= c)))


_G_CQ, _G_CKV, _G_QB, _G_QBR, _G_KB, _G_KBR, _G_VB, _G_QI, _G_QIR, _G_XC = [256 * i for i in range(10)]
_G_A = 2560
_G_AR = 2688
W1_COLS = 2816


def _proj_body(x_ref, w1_ref, wq_ref, wuk_ref, gq_ref, gkv_ref, c64_ref, s64_ref, ca_ref, sa_ref, cq_ref, sq_ref,
               qc_ref, kc_ref, ckv_ref, kb_ref, vb_ref, aux_ref, kbb_ref, vbb_ref, kib_ref, qbb_ref, qih_ref,
               xc_ref):
    x = x_ref[0].astype(BF16)

    def proj(lo, width):
        return _dot(x, w1_ref[:, lo:lo + width])

    c64 = c64_ref[...]
    s64 = s64_ref[...]
    c256 = jnp.concatenate([c64, c64], axis=-1)
    s256 = jnp.concatenate([s64, s64], axis=-1)

    ckv_n = _rms(proj(_G_CKV, 256), gkv_ref[...])
    ckv_ref[0] = ckv_n

    qb = proj(_G_QB, 256) * c256 + proj(_G_QBR, 256) * s256
    qbb_ref[0] = qb.astype(BF16)
    kb = proj(_G_KB, 256) * c256 + proj(_G_KBR, 256) * s256
    kb_ref[0] = kb
    kbb_ref[0] = kb.astype(BF16)
    vb = proj(_G_VB, 256)
    vb_ref[0] = vb
    vbb_ref[0] = vb.astype(BF16)
    qi = proj(_G_QI, 256) * c256 + proj(_G_QIR, 256) * s256
    for h in range(H_IDX):
        qih_ref[0, h] = qi[:, h * D_IDX:(h + 1) * D_IDX].astype(BF16)
    xc_ref[0] = proj(_G_XC, 256)

    aux = proj(_G_A, LANES) * ca_ref[...] + proj(_G_AR, LANES) * sa_ref[...]
    aux_ref[0] = aux
    kib_ref[0] = aux[:, :D_IDX].astype(BF16)
    lane = lax.broadcasted_iota(I32, aux.shape, 1)
    kr_pad = jnp.where((lane >= ROPE_LANE0) & (lane < ROPE_LANE0 + ROPE_A), aux, 0.0)
    kc_ref[0] = jnp.concatenate([ckv_n, kr_pad], axis=-1).astype(BF16)

    cqn = _rms(proj(_G_CQ, 256), gq_ref[...]).astype(BF16)
    cq_t = cq_ref[...]
    sq_t = sq_ref[...]
    for h in range(H_A):
        q_nope = _dot(cqn, wq_ref[:, h * LANES:(h + 1) * LANES]).astype(BF16)
        q_lat = _dot(q_nope, wuk_ref[h])
        q_rope = (_dot(cqn, wq_ref[:, (H_A + h) * LANES:(H_A + h + 1) * LANES]) * cq_t
                  + _dot(cqn, wq_ref[:, (2 * H_A + h) * LANES:(2 * H_A + h + 1) * LANES]) * sq_t)
        qc_ref[0, h] = jnp.concatenate([q_lat, q_rope], axis=-1).astype(BF16)


def _proj_call(x, wts, tabs, tm):
    b, t, d = x.shape
    grid = (b, t // tm)
    tok = lambda w, dt: jax.ShapeDtypeStruct((b, t, w), dt)
    out_shape = (
        jax.ShapeDtypeStruct((b, H_A, t, QC_W), BF16),
        tok(QC_W, BF16),
        tok(KV_RANK, F32),
        tok(B_WIDTH, F32), tok(B_WIDTH, F32),
        tok(LANES, F32),
        tok(B_WIDTH, BF16), tok(B_WIDTH, BF16),
        tok(D_IDX, BF16),
        tok(B_WIDTH, BF16),
        jax.ShapeDtypeStruct((b, H_IDX, t, D_IDX), BF16),
        tok(C_WIDTH, F32),
    )
    tokspec = lambda w: pl.BlockSpec((1, tm, w), lambda bi, i: (bi, i, 0))
    headspec = lambda hh, w: pl.BlockSpec((1, hh, tm, w), lambda bi, i: (bi, 0, i, 0))
    full2 = lambda a: pl.BlockSpec(a.shape, lambda bi, i: (0, 0))
    full3 = lambda a: pl.BlockSpec(a.shape, lambda bi, i: (0, 0, 0))
    tabspec = pl.BlockSpec((tm, LANES), lambda bi, i: (i, 0))
    in_specs = [tokspec(d), full2(wts['w1']), full2(wts['wq']), full3(wts['wuk']), full2(wts['gq']), full2(wts['gkv'])]
    in_specs += [tabspec] * 6
    out_specs = (headspec(H_A, QC_W), tokspec(QC_W), tokspec(KV_RANK), tokspec(B_WIDTH), tokspec(B_WIDTH),
                 tokspec(LANES), tokspec(B_WIDTH), tokspec(B_WIDTH), tokspec(D_IDX), tokspec(B_WIDTH),
                 headspec(H_IDX, D_IDX), tokspec(C_WIDTH))
    return pl.pallas_call(
        _proj_body, out_shape=out_shape, grid=grid, in_specs=in_specs, out_specs=out_specs,
        compiler_params=_cparams(("parallel", "parallel")), name="proj",
    )(x, wts['w1'], wts['wq'], wts['wuk'], wts['gq'], wts['gkv'], *tabs)


def _flash_init(m_ref, l_ref, acc_ref):
    m_ref[...] = jnp.full(m_ref.shape, -jnp.inf, F32)
    l_ref[...] = jnp.zeros(l_ref.shape, F32)
    acc_ref[...] = jnp.zeros(acc_ref.shape, F32)


def _lane_tile(x, width):
    return x if width == LANES else jnp.concatenate([x] * (width // LANES), axis=1)


def _lane_fold(x):
    out = x[:, :LANES]
    for i in range(1, x.shape[1] // LANES):
        out = out + x[:, i * LANES:(i + 1) * LANES]
    return out


def _flash_step(q, k, v, keep, m_ref, l_ref, acc_ref, scale):
    s = _dot_nt(q, k) * (scale * LOG2E)
    if keep is not None:
        s = jnp.where(keep, s, -jnp.inf)
    _flash_update(s, v, m_ref, l_ref, acc_ref)


def _flash_update(s, v, m_ref, l_ref, acc_ref):
    m_prev = m_ref[...]
    m_new = jnp.maximum(m_prev, jnp.max(s, axis=-1, keepdims=True))
    m_use = jnp.where(m_new == -jnp.inf, 0.0, m_new)
    alpha = jnp.exp2(m_prev - m_use)
    p = jnp.exp2(s - _lane_tile(m_use, s.shape[1]))
    l_ref[...] = alpha * l_ref[...] + _lane_fold(p)
    acc_ref[...] = _lane_tile(alpha, v.shape[1]) * acc_ref[...] + _dot(p.astype(BF16), v)
    m_ref[...] = m_new


def _flash_result(l_ref, acc_ref):
    return acc_ref[...] * (1.0 / jnp.sum(l_ref[...], axis=-1, keepdims=True))


def _mla_body(qc_ref, kc_ref, o_ref, m_ref, l_ref, acc_ref, *, tq):
    qi = pl.program_id(1)
    q = qc_ref[0].reshape(H_A * tq, QC_W)
    _flash_init(m_ref, l_ref, acc_ref)

    def step(j, keep):
        k = kc_ref[0, pl.ds(pl.multiple_of(j * tq, tq), tq), :]
        _flash_step(q, k, k[:, :KV_RANK], keep, m_ref, l_ref, acc_ref, MLA_SCALE)

    def pair_step(p, carry):
        step(2 * p, None)
        step(2 * p + 1, None)
        return carry

    lax.fori_loop(0, qi // 2, pair_step, 0)

    @pl.when(qi % 2 == 1)
    def _():
        step(qi - 1, None)

    row = lax.broadcasted_iota(I32, (H_A * tq, tq), 0)
    col = lax.broadcasted_iota(I32, (H_A * tq, tq), 1)
    step(qi, col <= (row & (tq - 1)))
    o_ref[0] = _flash_result(l_ref, acc_ref).reshape(H_A, tq, KV_RANK).astype(BF16)


def _mla_call(qc, kc, tq):
    b, _, t, _ = qc.shape
    return pl.pallas_call(
        functools.partial(_mla_body, tq=tq),
        out_shape=jax.ShapeDtypeStruct((b, H_A, t, KV_RANK), BF16),
        grid=(b, t // tq),
        in_specs=[pl.BlockSpec((1, H_A, tq, QC_W), lambda bi, i: (bi, 0, i, 0)),
                  pl.BlockSpec((1, t, QC_W), lambda bi, i: (bi, 0, 0))],
        out_specs=pl.BlockSpec((1, H_A, tq, KV_RANK), lambda bi, i: (bi, 0, i, 0)),
        scratch_shapes=[pltpu.VMEM((H_A * tq, LANES), F32), pltpu.VMEM((H_A * tq, LANES), F32),
                        pltpu.VMEM((H_A * tq, KV_RANK), F32)],
        compiler_params=_cparams(("parallel", "parallel")), name="mla_prompt",
    )(qc, kc)


def _head_rows(q):
    lane = lax.broadcasted_iota(I32, q.shape, 1)
    return jnp.concatenate([jnp.where((lane >= h * DH_B) & (lane < (h + 1) * DH_B), q, 0.0) for h in range(H_B)],
                           axis=0)


def _head_lanes(o, r):
    lane = lax.broadcasted_iota(I32, (r, o.shape[1]), 1)
    out = jnp.zeros((r, o.shape[1]), F32)
    for h in range(H_B):
        out = jnp.where((lane >= h * DH_B) & (lane < (h + 1) * DH_B), o[h * r:(h + 1) * r], out)
    return out


def _dsa_body(qih_ref, aux_ref, kib_ref, qbb_ref, kbb_ref, vbb_ref, ob_ref, sc_ref, t_ref, cut_ref, m_ref, l_ref,
              acc_ref, *, tq, ck, topk, idx_bits):
    qi = pl.program_id(1)
    nk = (qi * tq) // ck + 1
    q4 = qih_ref[0].reshape(H_IDX * tq, D_IDX)
    w_t = aux_ref[0].T[AUX_WI0:AUX_WI0 + H_IDX, :] * (H_IDX ** -0.5)
    krow = lax.broadcasted_iota(I32, (ck, tq), 0)
    qpos = qi * tq + lax.broadcasted_iota(I32, (ck, tq), 1)

    def kpos_of(c):
        return c * ck + krow

    def score_chunk(c, carry):
        kc = kib_ref[0, pl.ds(pl.multiple_of(c * ck, ck), ck), :]
        d_t = _dot_nt(kc, q4) * (D_IDX ** -0.5)
        sc = jnp.zeros((ck, tq), F32)
        for h in range(H_IDX):
            sc = sc + jnp.maximum(d_t[:, h * tq:(h + 1) * tq], 0.0) * w_t[h:h + 1, :]
        sc_ref[c] = jnp.where(kpos_of(c) <= qpos, sc, -jnp.inf)
        return carry

    lax.fori_loop(0, nk, score_chunk, 0)

    kf = jnp.float32(topk)

    def search(n):
        def count(pred):
            part = jnp.zeros((8, tq), F32)
            for c in range(n):
                part = part + jnp.sum(jnp.where(pred(c), 1.0, 0.0).reshape(ck // 8, 8, tq), axis=0)
            return jnp.sum(part, axis=0, keepdims=True)

        def value_step(i, key):
            cand = key + (jnp.int32(1) << (jnp.int32(31) - i))
            cand_f = _key_to_float(cand)
            return jnp.where(count(lambda c: sc_ref[c] >= cand_f) >= kf, cand, key)

        t = _kth_to_float(lax.fori_loop(0, 32, value_step, jnp.full((1, tq), INT_MIN, I32)))
        need = kf - count(lambda c: sc_ref[c] > t)

        def tie(c):
            return (sc_ref[c] == t) & (kpos_of(c) <= qpos)

        t_ref[0:1, :] = t
        cut_ref[0:1, :] = jnp.full((1, tq), 2 ** idx_bits, I32)
        surplus = jnp.max(count(tie) - need)

        @pl.when(surplus > 0.0)
        def _():
            def index_step(i, cut):
                cand = cut + (jnp.int32(1) << (jnp.int32(idx_bits - 1) - i))
                return jnp.where(count(lambda c: tie(c) & (kpos_of(c) < cand)) < need, cand, cut)

            cut_ref[0:1, :] = lax.fori_loop(0, idx_bits, index_step, jnp.zeros((1, tq), I32))

    t_ref[0:1, :] = jnp.full((1, tq), -jnp.inf, F32)
    cut_ref[0:1, :] = jnp.zeros((1, tq), I32)
    for n in range(1, sc_ref.shape[0] + 1):
        pl.when((nk == n) & ((qi + 1) * tq > topk))(functools.partial(search, n))
    t = t_ref[0:1, :]
    cut = cut_ref[0:1, :]

    qm = _head_rows(qbb_ref[0].astype(F32)).astype(BF16)
    _flash_init(m_ref, l_ref, acc_ref)

    def attend(c, carry):
        sc = sc_ref[c]
        sel_t = (sc > t) | ((sc == t) & (kpos_of(c) <= qpos) & (kpos_of(c) <= cut))
        keep = jnp.where(sel_t, 1.0, 0.0).T
        keep = jnp.concatenate([keep] * H_B, axis=0) > 0.5
        rows = pl.ds(pl.multiple_of(c * ck, ck), ck)
        _flash_step(qm, kbb_ref[0, rows, :], vbb_ref[0, rows, :], keep, m_ref, l_ref, acc_ref, DH_B ** -0.5)
        return carry

    lax.fori_loop(0, nk, attend, 0)
    ob_ref[0] = _head_lanes(_flash_result(l_ref, acc_ref), tq).astype(BF16)


def _dsa_call(qih, aux, kib, qbb, kbb, vbb, tq, topk):
    b, t, _ = qbb.shape
    ck = _pick_tile(t, 256)
    idx_bits = max(1, int(np.ceil(np.log2(t))))
    whole = lambda w: pl.BlockSpec((1, t, w), lambda bi, i: (bi, 0, 0))
    tile = lambda w: pl.BlockSpec((1, tq, w), lambda bi, i: (bi, i, 0))
    return pl.pallas_call(
        functools.partial(_dsa_body, tq=tq, ck=ck, topk=topk, idx_bits=idx_bits),
        out_shape=jax.ShapeDtypeStruct((b, t, B_WIDTH), BF16),
        grid=(b, t // tq),
        in_specs=[pl.BlockSpec((1, H_IDX, tq, D_IDX), lambda bi, i: (bi, 0, i, 0)), tile(LANES), whole(D_IDX),
                  tile(B_WIDTH), whole(B_WIDTH), whole(B_WIDTH)],
        out_specs=tile(B_WIDTH),
        scratch_shapes=[pltpu.VMEM((t // ck, ck, tq), F32), pltpu.VMEM((8, tq), F32), pltpu.VMEM((8, tq), I32),
                        pltpu.VMEM((H_B * tq, LANES), F32),
                        pltpu.VMEM((H_B * tq, LANES), F32), pltpu.VMEM((H_B * tq, B_WIDTH), F32)],
        compiler_params=_cparams(("parallel", "parallel")), name="dsa_prompt",
    )(qih, aux, kib, qbb, kbb, vbb)


def _pool_delta(win_sums, cnts, xcur):
    lane = lax.broadcasted_iota(I32, xcur.shape, 1)
    mean = jnp.zeros(xcur.shape, F32)
    for g, w in enumerate(POOL_WINDOWS):
        in_group = (lane >= g * POOL_CH) & (lane < (g + 1) * POOL_CH)
        mean = jnp.where(in_group, win_sums[w] / cnts[w], mean)
    return mean - xcur


def _merge_tail(o_lat_heads, ob, delta, x, wuv_ref, woa_ref, wob_ref, woc_ref, wp_ref, cs_ref, g_ref, b_ref):
    oc = _dot(delta.astype(BF16), wp_ref[...]) * cs_ref[...]
    acc = _dot(ob, wob_ref[...]) + _dot(oc.astype(BF16), woc_ref[...])
    for h in range(H_A):
        oa = _dot(o_lat_heads(h), wuv_ref[h]).astype(BF16)
        acc = acc + _dot(oa, woa_ref[h])
    return _layer_norm(ALPHA * x + acc, g_ref[...], b_ref[...])


def _merge_body(ol_ref, ob_ref, xc_ref, buf_ref, x_ref, wuv_ref, woa_ref, wob_ref, woc_ref, wp_ref, cs_ref, g_ref,
                b_ref, h_ref, ext_ref, *, tm):
    i = pl.program_id(1)
    start = pl.multiple_of(i * tm, tm)

    @pl.when(i == 0)
    def _():
        ext_ref[0:HALO, :] = buf_ref[0]

    @pl.when(i > 0)
    def _():
        ext_ref[0:HALO, :] = xc_ref[0, pl.ds(start - HALO, HALO), :]

    xcur = xc_ref[0, pl.ds(start, tm), :]
    ext_ref[HALO:HALO + tm, :] = xcur
    run = xcur
    win_sums = {}
    for k in range(1, max(POOL_WINDOWS)):
        run = run + ext_ref[HALO - k:HALO - k + tm, :]
        if k + 1 in POOL_WINDOWS:
            win_sums[k + 1] = run
    pos1 = (start + 1 + lax.broadcasted_iota(I32, (tm, 1), 0)).astype(F32)
    cnts = {w: jnp.minimum(jnp.float32(w), pos1) for w in POOL_WINDOWS}
    delta = _pool_delta(win_sums, cnts, xcur)
    h_ref[0] = _merge_tail(lambda h: ol_ref[0, h], ob_ref[0], delta, x_ref[0], wuv_ref, woa_ref, wob_ref, woc_ref,
                           wp_ref, cs_ref, g_ref, b_ref)


def _merge_call(o_lat, ob, xc, buf, x, mw, tm):
    b, t, d = x.shape
    tile = lambda w: pl.BlockSpec((1, tm, w), lambda bi, i: (bi, i, 0))
    full = lambda a: pl.BlockSpec(a.shape, lambda bi, i: (0,) * a.ndim)
    wnames = ('wuv', 'woa', 'wob', 'woc', 'wp', 'cs', 'g1', 'b1')
    return pl.pallas_call(
        functools.partial(_merge_body, tm=tm),
        out_shape=jax.ShapeDtypeStruct((b, t, d), F32),
        grid=(b, t // tm),
        in_specs=[pl.BlockSpec((1, H_A, tm, KV_RANK), lambda bi, i: (bi, 0, i, 0)), tile(B_WIDTH),
                  pl.BlockSpec((1, t, C_WIDTH), lambda bi, i: (bi, 0, 0)),
                  pl.BlockSpec((1, HALO, C_WIDTH), lambda bi, i: (bi, 0, 0)), tile(d)]
                 + [full(mw[n]) for n in wnames],
        out_specs=tile(d),
        scratch_shapes=[pltpu.VMEM((HALO + tm, C_WIDTH), F32)],
        compiler_params=_cparams(("parallel", "arbitrary")), name="merge_prompt",
    )(o_lat, ob, xc, buf, x, *[mw[n] for n in wnames])


def _merge_sample_body(ol_ref, ob_ref, ext_ref, x_ref, wuv_ref, woa_ref, wob_ref, woc_ref, wp_ref, cs_ref, g_ref,
                       b_ref, h_ref, *, n_seen):
    n = ext_ref.shape[1]
    xcur = ext_ref[HALO - 1]
    run = xcur
    win_sums = {}
    for k in range(1, max(POOL_WINDOWS)):
        run = run + ext_ref[HALO - 1 - k]
        if k + 1 in POOL_WINDOWS:
            win_sums[k + 1] = run
    cnts = {w: jnp.full((n, 1), min(w, n_seen), F32) for w in POOL_WINDOWS}
    delta = _pool_delta(win_sums, cnts, xcur)
    h_ref[...] = _merge_tail(lambda h: ol_ref[h], ob_ref[...], delta, x_ref[...], wuv_ref, woa_ref, wob_ref, woc_ref,
                             wp_ref, cs_ref, g_ref, b_ref)


def _merge_sample_call(o_lat_h, ob, ext, x, mw, n_seen):
    n, d = x.shape
    wnames = ('wuv', 'woa', 'wob', 'woc', 'wp', 'cs', 'g1', 'b1')
    return pl.pallas_call(
        functools.partial(_merge_sample_body, n_seen=n_seen),
        out_shape=jax.ShapeDtypeStruct((n, d), F32), name="merge_sample",
        compiler_params=pltpu.CompilerParams(vmem_limit_bytes=VMEM_LIMIT),
    )(o_lat_h, ob, ext, x, *[mw[n_] for n_ in wnames])


def _swiglu_partial(xb, wg, wu, wd):
    a = _dot(xb, wg)
    u = _dot(xb, wu)
    hm = (a / (1.0 + jnp.exp(-a))) * u
    return _dot(hm.astype(BF16), wd)


def _ffn_body(h_ref, wg_ref, wu_ref, wd_ref, g_ref, b_ref, o_ref, xb_ref, acc_ref):
    k = pl.program_id(1)

    @pl.when(k == 0)
    def _():
        xb_ref[...] = h_ref[...].astype(BF16)
        acc_ref[...] = jnp.zeros(acc_ref.shape, F32)

    acc_ref[...] += _swiglu_partial(xb_ref[...], wg_ref[...], wu_ref[...], wd_ref[...])

    @pl.when(k == pl.num_programs(1) - 1)
    def _():
        o_ref[...] = _layer_norm(ALPHA * h_ref[...] + acc_ref[...], g_ref[...], b_ref[...])


def _ffn_call(h, wg, wu, wd, g, bta, tm, tf):
    n, d = h.shape
    f = wg.shape[1]
    return pl.pallas_call(
        _ffn_body, out_shape=jax.ShapeDtypeStruct((n, d), F32), grid=(n // tm, f // tf),
        in_specs=[pl.BlockSpec((tm, d), lambda i, k: (i, 0)), pl.BlockSpec((d, tf), lambda i, k: (0, k)),
                  pl.BlockSpec((d, tf), lambda i, k: (0, k)), pl.BlockSpec((tf, d), lambda i, k: (k, 0)),
                  pl.BlockSpec((1, d), lambda i, k: (0, 0)), pl.BlockSpec((1, d), lambda i, k: (0, 0))],
        out_specs=pl.BlockSpec((tm, d), lambda i, k: (i, 0)),
        scratch_shapes=[pltpu.VMEM((tm, d), BF16), pltpu.VMEM((tm, d), F32)],
        compiler_params=_cparams(("parallel", "arbitrary")), name="ffn_dense",
    )(h, wg, wu, wd, g, bta)


def _router_body(h_ref, r_ref, gate_ref):
    logits = jnp.dot(h_ref[...], r_ref[...], precision=lax.Precision.HIGHEST, preferred_element_type=F32)
    lane = lax.broadcasted_iota(I32, logits.shape, 1).astype(F32)
    logits = jnp.where(lane < N_EXPERTS, logits, -jnp.inf)
    v1 = jnp.max(logits, axis=-1, keepdims=True)
    i1 = jnp.min(jnp.where(logits == v1, lane, float(LANES)), axis=-1, keepdims=True)
    rest = jnp.where(lane == i1, -jnp.inf, logits)
    v2 = jnp.max(rest, axis=-1, keepdims=True)
    i2 = jnp.min(jnp.where(rest == v2, lane, float(LANES)), axis=-1, keepdims=True)
    e2 = jnp.exp(v2 - v1)
    g1 = 1.0 / (1.0 + e2)
    g2 = e2 / (1.0 + e2)
    dense = jnp.where(lane == i1, g1, jnp.where(lane == i2, g2, 0.0))
    for off, val in enumerate((i1, i2, g1, g2)):
        dense = jnp.where(lane == float(ROUTE_LANE0 + off), val, dense)
    gate_ref[...] = dense


def _router_call(h, router_pad, tm):
    n, d = h.shape
    return pl.pallas_call(
        _router_body, out_shape=jax.ShapeDtypeStruct((n, LANES), F32), grid=(n // tm,),
        in_specs=[pl.BlockSpec((tm, d), lambda i: (i, 0)), pl.BlockSpec((d, LANES), lambda i: (0, 0))],
        out_specs=pl.BlockSpec((tm, LANES), lambda i: (i, 0)),
        compiler_params=_cparams(("parallel",)), name="moe_router",
    )(h, router_pad)


def _moe_body(h_ref, gate_ref, wg_ref, wu_ref, wd_ref, g_ref, b_ref, o_ref, xb_ref, acc_ref):
    e = pl.program_id(1)
    k = pl.program_id(2)

    @pl.when((e == 0) & (k == 0))
    def _():
        xb_ref[...] = h_ref[...].astype(BF16)
        acc_ref[...] = jnp.zeros(acc_ref.shape, F32)

    gates = gate_ref[...]
    lane = lax.broadcasted_iota(I32, gates.shape, 1)
    ge = jnp.sum(jnp.where(lane == e, gates, 0.0), axis=-1, keepdims=True)
    acc_ref[...] += ge * _swiglu_partial(xb_ref[...], wg_ref[0], wu_ref[0], wd_ref[0])

    @pl.when((e == pl.num_programs(1) - 1) & (k == pl.num_programs(2) - 1))
    def _():
        o_ref[...] = _layer_norm(ALPHA * h_ref[...] + acc_ref[...], g_ref[...], b_ref[...])


def _moe_call(h, gates, wg, wu, wd, g, bta, tm, tf):
    n, d = h.shape
    ne, _, f = wg.shape
    return pl.pallas_call(
        _moe_body, out_shape=jax.ShapeDtypeStruct((n, d), F32), grid=(n // tm, ne, f // tf),
        in_specs=[pl.BlockSpec((tm, d), lambda i, e, k: (i, 0)), pl.BlockSpec((tm, LANES), lambda i, e, k: (i, 0)),
                  pl.BlockSpec((1, d, tf), lambda i, e, k: (e, 0, k)),
                  pl.BlockSpec((1, d, tf), lambda i, e, k: (e, 0, k)),
                  pl.BlockSpec((1, tf, d), lambda i, e, k: (e, k, 0)),
                  pl.BlockSpec((1, d), lambda i, e, k: (0, 0)), pl.BlockSpec((1, d), lambda i, e, k: (0, 0))],
        out_specs=pl.BlockSpec((tm, d), lambda i, e, k: (i, 0)),
        scratch_shapes=[pltpu.VMEM((tm, d), BF16), pltpu.VMEM((tm, d), F32)],
        compiler_params=_cparams(("parallel", "arbitrary", "arbitrary")), name="ffn_moe",
    )(h, gates, wg, wu, wd, g, bta)


def _route_plan(route, tm):
    n = route.shape[0]
    ea = route[:, ROUTE_LANE0:ROUTE_LANE0 + 2].astype(I32).T.reshape(-1)
    ga = route[:, ROUTE_LANE0 + 2:ROUTE_LANE0 + 4].T.reshape(-1)
    onehot = (ea[:, None] == jnp.arange(N_EXPERTS, dtype=I32)[None, :]).astype(I32)
    csum = jnp.cumsum(onehot, axis=0)
    counts = csum[-1]
    rank = jnp.sum(onehot * csum, axis=1) - 1
    tiles_e = (counts + tm - 1) // tm
    tile_end = jnp.cumsum(tiles_e)
    tile_off = tile_end - tiles_e
    pos = tile_off[ea] * tm + rank
    n_tiles = (2 * n) // tm + N_EXPERTS
    pair = jnp.arange(2 * n, dtype=I32)
    pair_of = jnp.full((n_tiles * tm,), -1, I32).at[pos].set(pair)
    real = pair_of >= 0
    tok_of = jnp.where(real, pair_of % n, 0)
    slot_row = jnp.arange(n_tiles * tm, dtype=I32)
    spare = 2 * n + ((slot_row // tm) % 2) * tm + slot_row % tm
    dst_of = jnp.where(real, pair_of, spare)
    gate_of = jnp.where(real, ga[jnp.maximum(pair_of, 0)], 0.0)
    tile_ids = jnp.arange(n_tiles, dtype=I32)
    te = jnp.minimum(jnp.sum((tile_ids[:, None] >= tile_end[None, :]).astype(I32), axis=1), N_EXPERTS - 1)
    nv = jnp.clip(counts[te] - (tile_ids - tile_off[te]) * tm, 0, tm)
    return (tok_of.reshape(n_tiles, 1, tm), dst_of.reshape(n_tiles, 1, tm), gate_of.reshape(n_tiles * tm, 1),
            te.astype(I32), nv.astype(I32))


def _moe_routed_body(te_ref, nv_ref, tokc_ref, tokn_ref, dstp_ref, dstc_ref, gate_ref, h_hbm, wg_ref, wu_ref, wd_ref,
                     y_hbm, xbuf, xb_ref, acc_ref, ybuf, gsem, ssem, *, tm, rows_per_step):
    i = pl.program_id(0)
    k = pl.program_id(1)
    n_t = pl.num_programs(0)
    slot = i % 2
    other = 1 - slot

    def gather_copy(tok_ref, r, buf_slot):
        return pltpu.make_async_copy(h_hbm.at[pl.ds(tok_ref[0, 0, r], 1), :], xbuf.at[buf_slot, pl.ds(r, 1), :],
                                     gsem.at[buf_slot])

    def scatter_copy(dst_ref, r, buf_slot):
        return pltpu.make_async_copy(ybuf.at[buf_slot, pl.ds(r, 1), :], y_hbm.at[pl.ds(dst_ref[0, 0, r], 1), :],
                                     ssem.at[buf_slot])

    def start_rows(copy_of):
        def body(r, carry):
            copy_of(r).start()
            return carry
        lax.fori_loop(0, tm, body, 0, unroll=8)

    def wait_gather(buf_slot):
        pltpu.make_async_copy(h_hbm.at[pl.ds(0, tm), :], xbuf.at[buf_slot], gsem.at[buf_slot]).wait()

    def wait_scatter(buf_slot):
        pltpu.make_async_copy(ybuf.at[buf_slot], y_hbm.at[pl.ds(0, tm), :], ssem.at[buf_slot]).wait()

    @pl.when((i == 0) & (k == 0))
    def _():
        start_rows(lambda r: gather_copy(tokc_ref, r, 0))
        ybuf[1] = jnp.zeros(ybuf.shape[1:], F32)

    @pl.when(k == 0)
    def _():
        wait_gather(slot)
        xb_ref[...] = xbuf[slot].astype(BF16)
        acc_ref[...] = jnp.zeros(acc_ref.shape, F32)

    base = k * rows_per_step
    for j in range(rows_per_step):
        gather_copy(tokn_ref, base + j, other).start()
        scatter_copy(dstp_ref, base + j, other).start()

    @pl.when(nv_ref[i] > 0)
    def _():
        acc_ref[...] += _swiglu_partial(xb_ref[...], wg_ref[0], wu_ref[0], wd_ref[0])

    @pl.when(k == pl.num_programs(1) - 1)
    def _():
        @pl.when(i > 0)
        def _():
            wait_scatter(slot)

        ybuf[slot] = acc_ref[...] * gate_ref[...]

        @pl.when(i == n_t - 1)
        def _():
            start_rows(lambda r: scatter_copy(dstc_ref, r, slot))
            wait_scatter(other)
            wait_scatter(slot)
            wait_gather(other)


def _moe_routed_call(h, route, wg, wu, wd, tm, tf):
    n, d = h.shape
    f = wg.shape[2]
    n_k = f // tf
    assert f % tf == 0 and tm % n_k == 0, (f, tf, tm)
    tok_of, dst_of, gate_of, te, nv = _route_plan(route, tm)
    n_tiles = tok_of.shape[0]
    first_prev = (2 * n + 2 * tm + jnp.arange(tm, dtype=I32)).reshape(1, 1, tm)
    dst_prev = jnp.concatenate([first_prev, dst_of[:-1]], axis=0)
    smem_row = lambda fn: pl.BlockSpec((1, 1, tm), fn, memory_space=pltpu.SMEM)
    cur = lambda i, k, te_, nv_: (i, 0, 0)
    grid_spec = pltpu.PrefetchScalarGridSpec(
        num_scalar_prefetch=2, grid=(n_tiles, n_k),
        in_specs=[smem_row(cur),
                  smem_row(lambda i, k, te_, nv_: (jnp.minimum(i + 1, n_tiles - 1), 0, 0)),
                  smem_row(cur), smem_row(cur),
                  pl.BlockSpec((tm, 1), lambda i, k, te_, nv_: (i, 0)),
                  pl.BlockSpec(memory_space=pl.ANY),
                  pl.BlockSpec((1, d, tf), lambda i, k, te_, nv_: (te_[i], 0, k)),
                  pl.BlockSpec((1, d, tf), lambda i, k, te_, nv_: (te_[i], 0, k)),
                  pl.BlockSpec((1, tf, d), lambda i, k, te_, nv_: (te_[i], k, 0))],
        out_specs=pl.BlockSpec(memory_space=pl.ANY),
        scratch_shapes=[pltpu.VMEM((2, tm, d), F32), pltpu.VMEM((tm, d), BF16), pltpu.VMEM((tm, d), F32),
                        pltpu.VMEM((2, tm, d), F32), pltpu.SemaphoreType.DMA((2,)), pltpu.SemaphoreType.DMA((2,))])
    return pl.pallas_call(
        functools.partial(_moe_routed_body, tm=tm, rows_per_step=tm // n_k),
        out_shape=jax.ShapeDtypeStruct((2 * n + 3 * tm, d), F32), grid_spec=grid_spec,
        compiler_params=_cparams(("arbitrary", "arbitrary")), name="ffn_moe_routed",
    )(te, nv, tok_of, tok_of, dst_prev, dst_of, gate_of, h, wg, wu, wd)


def _combine_body(h_ref, y0_ref, y1_ref, g_ref, b_ref, o_ref):
    o_ref[...] = _layer_norm(ALPHA * h_ref[...] + (y0_ref[...] + y1_ref[...]), g_ref[...], b_ref[...])


def _combine_call(h, y2, g, bta, tm):
    n, d = h.shape
    return pl.pallas_call(
        _combine_body, out_shape=jax.ShapeDtypeStruct((n, d), F32), grid=(n // tm,),
        in_specs=[pl.BlockSpec((tm, d), lambda i: (i, 0)), pl.BlockSpec((tm, d), lambda i: (i, 0)),
                  pl.BlockSpec((tm, d), lambda i: (i + n // tm, 0)),
                  pl.BlockSpec((1, d), lambda i: (0, 0)), pl.BlockSpec((1, d), lambda i: (0, 0))],
        out_specs=pl.BlockSpec((tm, d), lambda i: (i, 0)),
        compiler_params=_cparams(("parallel",)), name="moe_combine",
    )(h, y2, y2, g, bta)


def _stream_page_groups(pt_ref, layer, group, streams, sems, compute):
    b = pl.program_id(0)
    n_seq = pl.num_programs(0)
    n_groups = pt_ref.shape[1] // group
    ahead = STREAM_SLOTS - 1

    def start(seq, g, slot):
        for j in range(group):
            page = pt_ref[seq, g * group + j]
            for a, (hbm, buf) in enumerate(streams):
                pltpu.make_async_copy(hbm.at[layer, page], buf.at[slot, j], sems.at[a, slot]).start()

    def wait(slot):
        for a, (hbm, buf) in enumerate(streams):
            pltpu.make_async_copy(hbm.at[layer, pl.ds(0, group)], buf.at[slot], sems.at[a, slot]).wait()

    @pl.when(b == 0)
    def _():
        for g0 in range(ahead):
            start(0, g0, g0)

    def body(g, carry):
        slot = g % STREAM_SLOTS
        wait(slot)
        nxt = g + ahead
        nxt_slot = nxt % STREAM_SLOTS

        @pl.when(nxt < n_groups)
        def _():
            start(b, nxt, nxt_slot)

        @pl.when((nxt >= n_groups) & (b + 1 < n_seq))
        def _():
            start(b + 1, nxt - n_groups, nxt_slot)

        compute(g, slot)
        return carry

    lax.fori_loop(0, n_groups, body, 0)


def _mla_idx_sample_body(pt_ref, q_ref, knew_ref, qi_ref, w_ref, ck_hbm, kr_hbm, ki_hbm, o_ref, sc_ref,
                         ckbuf, krbuf, kibuf, sems, m_ref, l_ref, acc_ref, *, layer, group):
    _flash_init(m_ref, l_ref, acc_ref)
    q = q_ref[0]
    q_rope = q[:, KV_RANK + ROPE_LANE0:KV_RANK + ROPE_LANE0 + ROPE_A]
    w = w_ref[0] * (H_IDX ** -0.5)

    def compute(g, slot):
        ck = ckbuf[slot].reshape(group * PAGE_SIZE, KV_RANK).astype(BF16)
        kr_t = jnp.concatenate([krbuf[slot, j] for j in range(group)], axis=1).astype(BF16)
        s = (_dot_nt(q[:, :KV_RANK], ck) + _dot(q_rope, kr_t)) * (MLA_SCALE * LOG2E)
        _flash_update(s, ck, m_ref, l_ref, acc_ref)
        ki_t = jnp.concatenate([kibuf[slot, j] for j in range(group)], axis=1).astype(BF16)
        dots = _dot(qi_ref[0], ki_t) * (D_IDX ** -0.5)
        sc_ref[0, pl.ds(g, 1), :] = jnp.sum(jnp.maximum(dots, 0.0) * w, axis=0, keepdims=True)

    _stream_page_groups(pt_ref, layer, group, ((ck_hbm, ckbuf), (kr_hbm, krbuf), (ki_hbm, kibuf)), sems, compute)
    kn = knew_ref[0].astype(F32)
    s_n = jnp.sum(q.astype(F32) * kn, axis=-1, keepdims=True) * (MLA_SCALE * LOG2E)
    m_old = m_ref[...]
    m_fin = jnp.maximum(m_old, s_n)
    a = jnp.exp2(m_old - m_fin)
    p_n = jnp.exp2(s_n - m_fin)[:, 0:1]
    l_fin = jnp.sum(a * l_ref[...], axis=-1, keepdims=True) + p_n
    acc = _lane_tile(a, KV_RANK) * acc_ref[...] + p_n * kn[:, :KV_RANK]
    o_ref[0] = (acc * (1.0 / l_fin)).astype(BF16)


def _page_group(n_pages, most):
    g = _pick_tile(n_pages, most)
    while g > 1 and (n_pages // g) % STREAM_SLOTS:
        g //= 2
    assert (n_pages // g) % STREAM_SLOTS == 0, n_pages
    return g


def _mla_idx_sample_call(page_table, q, knew, qi, wi, cache_ckv, cache_kr_t, cache_ki_t, layer):
    n, n_pages = page_table.shape
    group = _page_group(n_pages, 16)
    n_groups = n_pages // group
    per_seq = lambda *blk: pl.BlockSpec((1,) + blk, lambda bi, pt: (bi,) + (0,) * len(blk))
    any_spec = pl.BlockSpec(memory_space=pl.ANY)
    grid_spec = pltpu.PrefetchScalarGridSpec(
        num_scalar_prefetch=1, grid=(n,),
        in_specs=[per_seq(H_A, QC_W), per_seq(1, QC_W), per_seq(H_IDX, D_IDX), per_seq(H_IDX, 1),
                  any_spec, any_spec, any_spec],
        out_specs=(per_seq(H_A, KV_RANK), per_seq(n_groups, group * PAGE_SIZE)),
        scratch_shapes=[pltpu.VMEM((STREAM_SLOTS, group, PAGE_SIZE, KV_RANK), F32),
                        pltpu.VMEM((STREAM_SLOTS, group, ROPE_A, PAGE_SIZE), F32),
                        pltpu.VMEM((STREAM_SLOTS, group, D_IDX, PAGE_SIZE), F32),
                        pltpu.SemaphoreType.DMA((3, STREAM_SLOTS)),
                        pltpu.VMEM((H_A, LANES), F32), pltpu.VMEM((H_A, LANES), F32),
                        pltpu.VMEM((H_A, KV_RANK), F32)])
    o_lat, score = pl.pallas_call(
        functools.partial(_mla_idx_sample_body, layer=layer, group=group),
        out_shape=(jax.ShapeDtypeStruct((n, H_A, KV_RANK), BF16),
                   jax.ShapeDtypeStruct((n, n_groups, group * PAGE_SIZE), F32)),
        grid_spec=grid_spec, compiler_params=_cparams(("arbitrary",)), name="mla_idx_sample",
    )(page_table, q, knew, qi, wi, cache_ckv, cache_kr_t, cache_ki_t)
    return o_lat, score.reshape(n, n_pages * PAGE_SIZE)


def _select_sample_body(sp_ref, qi_ref, w_ref, kin_ref, sel_ref, *, past, topk, idx_bits):
    n = sp_ref.shape[0]
    kn = kin_ref[...].astype(F32)
    s_new = jnp.zeros((n, 1), F32)
    for h in range(H_IDX):
        d = jnp.sum(qi_ref[h].astype(F32) * kn, axis=-1, keepdims=True) * (D_IDX ** -0.5)
        s_new = s_new + jnp.maximum(d, 0.0) * (w_ref[:, h:h + 1] * (H_IDX ** -0.5))
    lane = lax.broadcasted_iota(I32, (n, LANES), 1)
    tail = jnp.where(lane == 0, s_new, 0.0)
    score = jnp.concatenate([sp_ref[...], tail], axis=-1)
    kpos = lax.broadcasted_iota(I32, score.shape, 1)
    sel_ref[...] = jnp.where(_topk_mask(score, kpos <= past, kpos, topk, idx_bits), 1.0, 0.0)


def _select_sample_call(score_past, qi, wi, ki_new, topk):
    n, past = score_past.shape
    idx_bits = int(np.ceil(np.log2(past + LANES)))
    return pl.pallas_call(
        functools.partial(_select_sample_body, past=past, topk=topk, idx_bits=idx_bits),
        out_shape=jax.ShapeDtypeStruct((n, past + LANES), F32),
        compiler_params=pltpu.CompilerParams(vmem_limit_bytes=VMEM_LIMIT), name="select_sample",
    )(score_past, qi, wi, ki_new)


def _dsa_sample_body(pt_ref, q_ref, sel_ref, seln_ref, kn_ref, vn_ref, k_hbm, v_hbm, o_ref, kbuf, vbuf, sems,
                     m_ref, l_ref, acc_ref, *, layer, group):
    _flash_init(m_ref, l_ref, acc_ref)
    qcol = q_ref[0]
    scale = (DH_B ** -0.5) * LOG2E

    def compute(g, slot):
        keep = sel_ref[0, pl.ds(g, 1), :] > 0.5
        for h in range(H_B):
            qh = qcol[h * DH_B:(h + 1) * DH_B]
            k_t = jnp.concatenate([kbuf[slot, j, h] for j in range(group)], axis=1)
            s = jnp.sum(k_t * qh, axis=0, keepdims=True) * scale
            s = jnp.where(keep, s, -jnp.inf)
            m_prev = m_ref[h]
            m_new = jnp.maximum(m_prev, jnp.max(s, axis=-1, keepdims=True))
            m_use = jnp.where(m_new == -jnp.inf, 0.0, m_new)
            alpha = jnp.exp2(m_prev - m_use)
            p = jnp.exp2(s - _lane_tile(m_use, s.shape[1]))
            v_t = jnp.concatenate([vbuf[slot, j, h] for j in range(group)], axis=1)
            l_ref[h] = alpha * l_ref[h] + _lane_fold(p)
            acc_ref[h] = alpha * acc_ref[h] + _lane_fold(v_t * p)
            m_ref[h] = m_new

    _stream_page_groups(pt_ref, layer, group, ((k_hbm, kbuf), (v_hbm, vbuf)), sems, compute)
    new_kept = seln_ref[0][:, 0:1] > 0.5
    for h in range(H_B):
        rows = slice(h * DH_B, (h + 1) * DH_B)
        s_n = jnp.sum(kn_ref[0][rows] * qcol[rows], axis=0, keepdims=True) * scale
        s_n = jnp.where(new_kept, s_n, -jnp.inf)
        m_prev = m_ref[h]
        m_new = jnp.maximum(m_prev, s_n)
        m_use = jnp.where(m_new == -jnp.inf, 0.0, m_new)
        alpha = jnp.exp2(m_prev - m_use)
        p_n = jnp.exp2(s_n - m_use)[:, 0:1]
        l_tot = jnp.sum(alpha * l_ref[h], axis=-1, keepdims=True) + p_n
        acc = jnp.sum(alpha * acc_ref[h], axis=-1, keepdims=True) + p_n * vn_ref[0][rows]
        o_ref[0, rows, :] = acc * (1.0 / l_tot)


def _dsa_sample_call(page_table, q_col, sel, kn_col, vn_col, cache_k_t, cache_v_t, layer):
    n, n_pages = page_table.shape
    group = _page_group(n_pages, 8)
    n_groups = n_pages // group
    past = n_pages * PAGE_SIZE
    sel_past = sel[:, :past].reshape(n, n_groups, group * PAGE_SIZE)
    sel_new = sel[:, past:].reshape(n, 1, LANES)
    per_seq = lambda *blk: pl.BlockSpec((1,) + blk, lambda bi, pt: (bi,) + (0,) * len(blk))
    any_spec = pl.BlockSpec(memory_space=pl.ANY)
    grid_spec = pltpu.PrefetchScalarGridSpec(
        num_scalar_prefetch=1, grid=(n,),
        in_specs=[per_seq(B_WIDTH, 1), per_seq(n_groups, group * PAGE_SIZE), per_seq(1, LANES), per_seq(B_WIDTH, 1),
                  per_seq(B_WIDTH, 1), any_spec, any_spec],
        out_specs=per_seq(B_WIDTH, 1),
        scratch_shapes=[pltpu.VMEM((STREAM_SLOTS, group, H_B, DH_B, PAGE_SIZE), F32),
                        pltpu.VMEM((STREAM_SLOTS, group, H_B, DH_B, PAGE_SIZE), F32),
                        pltpu.SemaphoreType.DMA((2, STREAM_SLOTS)),
                        pltpu.VMEM((H_B, 1, LANES), F32), pltpu.VMEM((H_B, 1, LANES), F32),
                        pltpu.VMEM((H_B, DH_B, LANES), F32)])
    return pl.pallas_call(
        functools.partial(_dsa_sample_body, layer=layer, group=group),
        out_shape=jax.ShapeDtypeStruct((n, B_WIDTH, 1), F32), grid_spec=grid_spec,
        compiler_params=_cparams(("arbitrary",)), name="dsa_sample",
    )(page_table, q_col, sel_past, sel_new, kn_col, vn_col, cache_k_t, cache_v_t)


def _rot_cols(w, head_dim):
    k, n = w.shape
    wh = w.reshape(k, n // head_dim, 2, head_dim // 2)
    return jnp.concatenate([-wh[:, :, 1], wh[:, :, 0]], axis=-1).reshape(k, n)


def _prep_layer(w_in, a_q_norm, a_kv_norm, a_w_uq, a_w_uk, a_w_uv, c_w_pool, c_scale, w_out, ln_g, ln_b):
    offs = np.cumsum((0,) + IN_SPLITS)
    cq, ckv, kr, qb, kb, vb, qi, ki, wi, xc = [w_in[:, offs[i]:offs[i + 1]] for i in range(len(IN_SPLITS))]
    d = w_in.shape[0]
    zeros = lambda n: jnp.zeros((d, n), F32)
    grp_a = jnp.concatenate([ki, kr, wi, zeros(LANES - D_IDX - ROPE_A - H_IDX)], axis=1)
    grp_ar = jnp.concatenate([_rot_cols(ki, D_IDX), _rot_cols(kr, ROPE_A), zeros(LANES - D_IDX - ROPE_A)], axis=1)
    w1 = jnp.concatenate([cq, ckv, qb, _rot_cols(qb, DH_B), kb, _rot_cols(kb, DH_B), vb, qi, _rot_cols(qi, D_IDX),
                          xc, grp_a, grp_ar], axis=1).astype(BF16)
    uq = a_w_uq.reshape(Q_RANK, H_A, NOPE_A + ROPE_A)
    nope = jnp.pad(uq[:, :, :NOPE_A], ((0, 0), (0, 0), (0, LANES - NOPE_A)))
    rope = uq[:, :, NOPE_A:]
    rope_rot = jnp.concatenate([-rope[..., ROPE_A // 2:], rope[..., :ROPE_A // 2]], axis=-1)
    pad_rope = lambda r: jnp.pad(r, ((0, 0), (0, 0), (ROPE_LANE0, LANES - ROPE_LANE0 - ROPE_A)))
    wq = jnp.concatenate([nope.reshape(Q_RANK, -1), pad_rope(rope).reshape(Q_RANK, -1),
                          pad_rope(rope_rot).reshape(Q_RANK, -1)], axis=1).astype(BF16)
    wuk = jnp.pad(jnp.transpose(a_w_uk, (1, 2, 0)), ((0, 0), (0, LANES - NOPE_A), (0, 0))).astype(BF16)
    proj_w = dict(w1=w1, wq=wq, wuk=wuk, gq=a_q_norm.reshape(1, -1), gkv=a_kv_norm.reshape(1, -1))
    wuv = jnp.pad(jnp.transpose(a_w_uv, (1, 0, 2)), ((0, 0), (0, 0), (0, LANES - V_A))).astype(BF16)
    woa = jnp.pad(w_out[:A_WIDTH].reshape(H_A, V_A, -1), ((0, 0), (0, LANES - V_A), (0, 0))).astype(BF16)
    wp = jnp.zeros((C_WIDTH, C_WIDTH), F32)
    for g in range(len(POOL_WINDOWS)):
        wp = wp.at[g * POOL_CH:(g + 1) * POOL_CH, g * POOL_CH:(g + 1) * POOL_CH].set(c_w_pool[g])
    merge_w = dict(wuv=wuv, woa=woa, wob=w_out[A_WIDTH:A_WIDTH + B_WIDTH].astype(BF16),
                   woc=w_out[A_WIDTH + B_WIDTH:].astype(BF16), wp=wp.astype(BF16), cs=c_scale.reshape(1, -1),
                   g1=ln_g.reshape(1, -1), b1=ln_b.reshape(1, -1))
    return proj_w, merge_w


def _rope_tables(pos):
    posf = pos.astype(F32)[:, None]

    def cs(dim):
        inv = ROPE_THETA ** (-jnp.arange(0, dim, 2, dtype=F32) / dim)
        ang = posf * inv[None, :]
        c, s = jnp.cos(ang), jnp.sin(ang)
        return jnp.concatenate([c, c], -1), jnp.concatenate([s, s], -1)

    c64, s64 = cs(DH_B)
    c32, s32 = cs(ROPE_A)
    t = pos.shape[0]
    z = lambda n: jnp.zeros((t, n), F32)
    c_pair, s_pair = jnp.concatenate([c64, c64], -1), jnp.concatenate([s64, s64], -1)
    c_a = jnp.concatenate([c64, c32, jnp.ones((t, H_IDX), F32), z(LANES - D_IDX - ROPE_A - H_IDX)], -1)
    s_a = jnp.concatenate([s64, s32, z(LANES - D_IDX - ROPE_A)], -1)
    c_q = jnp.concatenate([z(ROPE_LANE0), c32, z(LANES - ROPE_LANE0 - ROPE_A)], -1)
    s_q = jnp.concatenate([z(ROPE_LANE0), s32, z(LANES - ROPE_LANE0 - ROPE_A)], -1)
    return c_pair, s_pair, c_a, s_a, c_q, s_q


def _aux_split(aux):
    return aux[..., D_IDX:D_IDX + ROPE_A], aux[..., :D_IDX]


def _mixers_prompt(x, proj_w, merge_w, tabs, tiles):
    b, t, _ = x.shape
    (qc, kc, ckv_n, kb, vb, aux, kbb, vbb, kib, qbb, qih, xc) = _proj_call(x, proj_w, tabs, tiles['proj'])
    o_lat = _mla_call(qc, kc, tiles['mla'])
    ob = _dsa_call(qih, aux, kib, qbb, kbb, vbb, tiles['dsa'], min(TOPK_MAX, t // 4))
    buf = jnp.zeros((b, HALO, C_WIDTH), F32)
    h = _merge_call(o_lat, ob, xc, buf, x, merge_w, tiles['merge'])
    kr_r, ki_r = _aux_split(aux)
    state = (ckv_n, kr_r, kb.reshape(b, t, H_B, DH_B), vb.reshape(b, t, H_B, DH_B), ki_r, xc[:, t - POOL_BUF:])
    return h, state


def _mixers_sample(x, proj_w, merge_w, tabs, caches, state_pool, page_table, layer):
    n = x.shape[0]
    c_ckv, c_kr, c_k, c_v, c_kidx = caches
    (qc, kc, ckv_n, kb, vb, aux, kbb, vbb, kib, qbb, qih, xc) = _proj_call(x.reshape(1, n, -1), proj_w, tabs, n)
    q = jnp.transpose(qc[0], (1, 0, 2))
    qi = jnp.transpose(qih[0], (1, 0, 2))
    wi = aux[0][:, AUX_WI0:AUX_WI0 + H_IDX]
    o_lat, score_past = _mla_idx_sample_call(page_table, q, kc[0][:, None, :], qi, wi[:, :, None], c_ckv, c_kr,
                                             c_kidx, layer)
    past = score_past.shape[1]
    sel = _select_sample_call(score_past, qih[0], wi, kib[0], min(TOPK_MAX, (past + 1) // 4))
    col = lambda a: a.astype(F32)[:, :, None]
    ob = _dsa_sample_call(page_table, col(qbb[0]), sel, col(kb[0]), col(vb[0]), c_k, c_v,
                          layer)[:, :, 0].astype(BF16)
    ext = jnp.concatenate([state_pool, xc[0][:, None, :]], axis=1)
    h = _merge_sample_call(jnp.transpose(o_lat, (1, 0, 2)), ob, jnp.transpose(ext, (1, 0, 2)), x, merge_w, past + 1)
    kr_r, ki_r = _aux_split(aux[0])
    state = (ckv_n[0][:, None], kr_r[:, None], kb[0].reshape(n, 1, H_B, DH_B), vb[0].reshape(n, 1, H_B, DH_B),
             ki_r[:, None], ext[:, HALO - POOL_BUF:])
    return h, state


def _pick_tile(n, pref):
    t = min(n, pref)
    while n % t:
        t //= 2
    return t


def kernel(x_prompt, x_sample, cache_a_ckv, cache_a_krope, cache_b_k, cache_b_v, cache_b_kidx, state_pool, page_table, w_in, a_q_norm, a_kv_norm, a_w_uq, a_w_uk, a_w_uv, c_w_pool, c_scale, w_out, ln1_g, ln1_b, ln2_g, ln2_b, ffn_w_gate, ffn_w_up, ffn_w_down, moe_router, moe_w_gate, moe_w_up, moe_w_down):
    bsz, seq, d = x_prompt.shape
    n_dec = x_sample.shape[0]
    depth = w_in.shape[0]
    n_pages = page_table.shape[1]
    past = n_pages * PAGE_SIZE
    caches = (cache_a_ckv, jnp.transpose(cache_a_krope, (0, 1, 3, 2)), jnp.transpose(cache_b_k, (0, 1, 3, 4, 2)),
              jnp.transpose(cache_b_v, (0, 1, 3, 4, 2)), jnp.transpose(cache_b_kidx, (0, 1, 3, 2)))
    tiles = dict(proj=_pick_tile(seq, 512), mla=_pick_tile(seq, 256), dsa=_pick_tile(seq, 128),
                 merge=_pick_tile(seq, 512))
    tabs_p = _rope_tables(jnp.arange(seq, dtype=I32))
    tabs_s = _rope_tables(jnp.full((n_dec,), past, I32))
    hp = x_prompt
    hs = x_sample.reshape(n_dec, d)
    new_p, new_s = [], []
    for l in range(depth):
        proj_w, merge_w = _prep_layer(w_in[l], a_q_norm[l], a_kv_norm[l], a_w_uq[l], a_w_uk[l], a_w_uv[l],
                                      c_w_pool[l], c_scale[l], w_out[l], ln1_g[l], ln1_b[l])
        hp, st_p = _mixers_prompt(hp, proj_w, merge_w, tabs_p, tiles)
        hs, st_s = _mixers_sample(hs, proj_w, merge_w, tabs_s, caches, state_pool[l], page_table, l)
        g2, b2 = ln2_g[l].reshape(1, -1), ln2_b[l].reshape(1, -1)
        hp2 = hp.reshape(bsz * seq, d)
        j = l // 2
        tm = _pick_tile(bsz * seq, 1024)
        if l % 2 == 0:
            wg, wu, wd = ffn_w_gate[j].astype(BF16), ffn_w_up[j].astype(BF16), ffn_w_down[j].astype(BF16)
            hp2 = _ffn_call(hp2, wg, wu, wd, g2, b2, tm, 512)
            hs = _ffn_call(hs, wg, wu, wd, g2, b2, n_dec, 512)
        else:
            wg, wu, wd = moe_w_gate[j].astype(BF16), moe_w_up[j].astype(BF16), moe_w_down[j].astype(BF16)
            router = jnp.pad(moe_router[j], ((0, 0), (0, LANES - N_EXPERTS)))
            y2 = _moe_routed_call(hp2, _router_call(hp2, router, tm), wg, wu, wd, _pick_tile(bsz * seq, MOE_ROW_TILE),
                                  MOE_FF_TILE)
            hp2 = _combine_call(hp2, y2, g2, b2, tm)
            hs = _moe_call(hs, _router_call(hs, router, n_dec), wg, wu, wd, g2, b2, n_dec, 512)
        hp = hp2.reshape(bsz, seq, d)
        new_p.append(st_p)
        new_s.append(st_s)
    outs_p = [jnp.stack(a) for a in zip(*new_p)]
    outs_s = [jnp.stack(a) for a in zip(*new_s)]
    return (hp, hs.reshape(n_dec, 1, d), *outs_p, *outs_s)
```

```python
import functools

import numpy as np
import jax
import jax.numpy as jnp
from jax import lax
from jax.experimental import pallas as pl
from jax.experimental.pallas import tpu as pltpu

F32 = jnp.float32
BF16 = jnp.bfloat16
I32 = jnp.int32

D_MODEL = 1024
DEPTH = 2
PAGE_SIZE = 128
H_A, NOPE_A, ROPE_A, V_A = 8, 64, 32, 64
Q_RANK, KV_RANK = 256, 256
MLA_SCALE = (NOPE_A + ROPE_A) ** -0.5
H_B, DH_B = 4, 64
H_IDX, D_IDX = 4, 64
TOPK_MAX = 256
POOL_WINDOWS = (2, 4, 8, 16)
POOL_CH = 64
C_WIDTH = len(POOL_WINDOWS) * POOL_CH
POOL_BUF = max(POOL_WINDOWS) - 1
A_WIDTH = H_A * V_A
B_WIDTH = H_B * DH_B
IN_SPLITS = (Q_RANK, KV_RANK, ROPE_A, B_WIDTH, B_WIDTH, B_WIDTH, H_IDX * D_IDX, D_IDX, H_IDX, C_WIDTH)
D_FF = 3584
N_EXPERTS = 8
ROPE_THETA = 10000.0
ALPHA = (2 * DEPTH) ** 0.25
LN_EPS = 1e-5
RMS_EPS = 1e-6

LANES = 128
HALO = 16
QC_W = KV_RANK + LANES
ROPE_LANE0 = 64
AUX_WI0 = 96
INT_MIN = -(2 ** 31)
LOG2E = 1.4426950408889634
ROUTE_LANE0 = N_EXPERTS
MOE_ROW_TILE = 512
MOE_FF_TILE = 896
STREAM_SLOTS = 4
VMEM_LIMIT = 56 * 1024 * 1024


def _cparams(sem):
    return pltpu.CompilerParams(dimension_semantics=sem, vmem_limit_bytes=VMEM_LIMIT)


def _rms(x, g):
    return x * lax.rsqrt(jnp.mean(x * x, -1, keepdims=True) + RMS_EPS) * g


def _layer_norm(x, g, b):
    mu = jnp.mean(x, -1, keepdims=True)
    xc = x - mu
    var = jnp.mean(xc * xc, -1, keepdims=True)
    return xc * lax.rsqrt(var + LN_EPS) * g + b


def _dot(a, b):
    return jnp.dot(a, b, preferred_element_type=F32)


def _dot_nt(a, b):
    return lax.dot_general(a, b, (((1,), (1,)), ((), ())), preferred_element_type=F32)


def _count(mask):
    return jnp.sum(jnp.where(mask, 1.0, 0.0), axis=-1, keepdims=True)


def _key_to_float(key):
    return lax.bitcast_convert_type(key ^ ((key >> 31) & jnp.int32(0x7FFFFFFF)), F32)


def _kth_to_float(key):
    return jnp.where(key == jnp.int32(INT_MIN), -jnp.inf, _key_to_float(key))


def _topk_mask(score, valid, kpos, k, idx_bits):
    score = jnp.where(valid, score, -jnp.inf)
    rows = score.shape[0]
    kf = jnp.float32(k)

    def value_step(i, t):
        cand = t + (jnp.int32(1) << (jnp.int32(31) - i))
        return jnp.where(_count(score >= _key_to_float(cand)) >= kf, cand, t)

    t = _kth_to_float(lax.fori_loop(0, 32, value_step, jnp.full((rows, 1), INT_MIN, I32)))
    gt = score > t
    eq = (score == t) & valid
    need = kf - _count(gt)

    def index_step(i, c):
        cand = c + (jnp.int32(1) << (jnp.int32(idx_bits - 1) - i))
        return jnp.where(_count(eq & (kpos < cand)) < need, cand, c)

    c = lax.fori_loop(0, idx_bits, index_step, jnp.zeros((rows, 1), I32))
    return valid & (gt | (eq & (kpos <= c)))


_G_CQ, _G_CKV, _G_QB, _G_QBR, _G_KB, _G_KBR, _G_VB, _G_QI, _G_QIR, _G_XC = [256 * i for i in range(10)]
_G_A = 2560
_G_AR = 2688
W1_COLS = 2816


def _proj_body(x_ref, w1_ref, wq_ref, wuk_ref, gq_ref, gkv_ref, c64_ref, s64_ref, ca_ref, sa_ref, cq_ref, sq_ref,
               qc_ref, kc_ref, ckv_ref, kb_ref, vb_ref, aux_ref, kbb_ref, vbb_ref, kib_ref, qbb_ref, qih_ref,
               xc_ref):
    x = x_ref[0].astype(BF16)

    def proj(lo, width):
        return _dot(x, w1_ref[:, lo:lo + width])

    c64 = c64_ref[...]
    s64 = s64_ref[...]
    c256 = jnp.concatenate([c64, c64], axis=-1)
    s256 = jnp.concatenate([s64, s64], axis=-1)

    ckv_n = _rms(proj(_G_CKV, 256), gkv_ref[...])
    ckv_ref[0] = ckv_n

    qb = proj(_G_QB, 256) * c256 + proj(_G_QBR, 256) * s256
    qbb_ref[0] = qb.astype(BF16)
    kb = proj(_G_KB, 256) * c256 + proj(_G_KBR, 256) * s256
    kb_ref[0] = kb
    kbb_ref[0] = kb.astype(BF16)
    vb = proj(_G_VB, 256)
    vb_ref[0] = vb
    vbb_ref[0] = vb.astype(BF16)
    qi = proj(_G_QI, 256) * c256 + proj(_G_QIR, 256) * s256
    for h in range(H_IDX):
        qih_ref[0, h] = qi[:, h * D_IDX:(h + 1) * D_IDX].astype(BF16)
    xc_ref[0] = proj(_G_XC, 256)

    aux = proj(_G_A, LANES) * ca_ref[...] + proj(_G_AR, LANES) * sa_ref[...]
    aux_ref[0] = aux
    kib_ref[0] = aux[:, :D_IDX].astype(BF16)
    lane = lax.broadcasted_iota(I32, aux.shape, 1)
    kr_pad = jnp.where((lane >= ROPE_LANE0) & (lane < ROPE_LANE0 + ROPE_A), aux, 0.0)
    kc_ref[0] = jnp.concatenate([ckv_n, kr_pad], axis=-1).astype(BF16)

    cqn = _rms(proj(_G_CQ, 256), gq_ref[...]).astype(BF16)
    cq_t = cq_ref[...]
    sq_t = sq_ref[...]
    for h in range(H_A):
        q_nope = _dot(cqn, wq_ref[:, h * LANES:(h + 1) * LANES]).astype(BF16)
        q_lat = _dot(q_nope, wuk_ref[h])
        q_rope = (_dot(cqn, wq_ref[:, (H_A + h) * LANES:(H_A + h + 1) * LANES]) * cq_t
                  + _dot(cqn, wq_ref[:, (2 * H_A + h) * LANES:(2 * H_A + h + 1) * LANES]) * sq_t)
        qc_ref[0, h] = jnp.concatenate([q_lat, q_rope], axis=-1).astype(BF16)


def _proj_call(x, wts, tabs, tm):
    b, t, d = x.shape
    grid = (b, t // tm)
    tok = lambda w, dt: jax.ShapeDtypeStruct((b, t, w), dt)
    out_shape = (
        jax.ShapeDtypeStruct((b, H_A, t, QC_W), BF16),
        tok(QC_W, BF16),
        tok(KV_RANK, F32),
        tok(B_WIDTH, F32), tok(B_WIDTH, F32),
        tok(LANES, F32),
        tok(B_WIDTH, BF16), tok(B_WIDTH, BF16),
        tok(D_IDX, BF16),
        tok(B_WIDTH, BF16),
        jax.ShapeDtypeStruct((b, H_IDX, t, D_IDX), BF16),
        tok(C_WIDTH, F32),
    )
    tokspec = lambda w: pl.BlockSpec((1, tm, w), lambda bi, i: (bi, i, 0))
    headspec = lambda hh, w: pl.BlockSpec((1, hh, tm, w), lambda bi, i: (bi, 0, i, 0))
    full2 = lambda a: pl.BlockSpec(a.shape, lambda bi, i: (0, 0))
    full3 = lambda a: pl.BlockSpec(a.shape, lambda bi, i: (0, 0, 0))
    tabspec = pl.BlockSpec((tm, LANES), lambda bi, i: (i, 0))
    in_specs = [tokspec(d), full2(wts['w1']), full2(wts['wq']), full3(wts['wuk']), full2(wts['gq']), full2(wts['gkv'])]
    in_specs += [tabspec] * 6
    out_specs = (headspec(H_A, QC_W), tokspec(QC_W), tokspec(KV_RANK), tokspec(B_WIDTH), tokspec(B_WIDTH),
                 tokspec(LANES), tokspec(B_WIDTH), tokspec(B_WIDTH), tokspec(D_IDX), tokspec(B_WIDTH),
                 headspec(H_IDX, D_IDX), tokspec(C_WIDTH))
    return pl.pallas_call(
        _proj_body, out_shape=out_shape, grid=grid, in_specs=in_specs, out_specs=out_specs,
        compiler_params=_cparams(("parallel", "parallel")), name="proj",
    )(x, wts['w1'], wts['wq'], wts['wuk'], wts['gq'], wts['gkv'], *tabs)


def _flash_init(m_ref, l_ref, acc_ref):
    m_ref[...] = jnp.full(m_ref.shape, -jnp.inf, F32)
    l_ref[...] = jnp.zeros(l_ref.shape, F32)
    acc_ref[...] = jnp.zeros(acc_ref.shape, F32)


def _lane_tile(x, width):
    return x if width == LANES else jnp.concatenate([x] * (width // LANES), axis=1)


def _lane_fold(x):
    out = x[:, :LANES]
    for i in range(1, x.shape[1] // LANES):
        out = out + x[:, i * LANES:(i + 1) * LANES]
    return out


def _flash_step(q, k, v, keep, m_ref, l_ref, acc_ref, scale):
    s = _dot_nt(q, k) * (scale * LOG2E)
    if keep is not None:
        s = jnp.where(keep, s, -jnp.inf)
    _flash_update(s, v, m_ref, l_ref, acc_ref)


def _flash_update(s, v, m_ref, l_ref, acc_ref):
    m_prev = m_ref[...]
    m_new = jnp.maximum(m_prev, jnp.max(s, axis=-1, keepdims=True))
    m_use = jnp.where(m_new == -jnp.inf, 0.0, m_new)
    alpha = jnp.exp2(m_prev - m_use)
    p = jnp.exp2(s - _lane_tile(m_use, s.shape[1]))
    l_ref[...] = alpha * l_ref[...] + _lane_fold(p)
    acc_ref[...] = _lane_tile(alpha, v.shape[1]) * acc_ref[...] + _dot(p.astype(BF16), v)
    m_ref[...] = m_new


def _flash_result(l_ref, acc_ref):
    return acc_ref[...] * (1.0 / jnp.sum(l_ref[...], axis=-1, keepdims=True))


def _mla_body(qc_ref, kc_ref, o_ref, m_ref, l_ref, acc_ref, *, tq):
    qi = pl.program_id(1)
    q = qc_ref[0].reshape(H_A * tq, QC_W)
    _flash_init(m_ref, l_ref, acc_ref)

    def step(j, keep):
        k = kc_ref[0, pl.ds(pl.multiple_of(j * tq, tq), tq), :]
        _flash_step(q, k, k[:, :KV_RANK], keep, m_ref, l_ref, acc_ref, MLA_SCALE)

    def pair_step(p, carry):
        k = kc_ref[0, pl.ds(pl.multiple_of(p * 2 * tq, 2 * tq), 2 * tq), :]
        _flash_step(q, k, k[:, :KV_RANK], None, m_ref, l_ref, acc_ref, MLA_SCALE)
        return carry

    lax.fori_loop(0, qi // 2, pair_step, 0)

    @pl.when(qi % 2 == 1)
    def _():
        step(qi - 1, None)

    row = lax.broadcasted_iota(I32, (H_A * tq, tq), 0)
    col = lax.broadcasted_iota(I32, (H_A * tq, tq), 1)
    step(qi, col <= (row & (tq - 1)))
    o_ref[0] = _flash_result(l_ref, acc_ref).reshape(H_A, tq, KV_RANK).astype(BF16)


def _mla_call(qc, kc, tq):
    b, _, t, _ = qc.shape
    return pl.pallas_call(
        functools.partial(_mla_body, tq=tq),
        out_shape=jax.ShapeDtypeStruct((b, H_A, t, KV_RANK), BF16),
        grid=(b, t // tq),
        in_specs=[pl.BlockSpec((1, H_A, tq, QC_W), lambda bi, i: (bi, 0, i, 0)),
                  pl.BlockSpec((1, t, QC_W), lambda bi, i: (bi, 0, 0))],
        out_specs=pl.BlockSpec((1, H_A, tq, KV_RANK), lambda bi, i: (bi, 0, i, 0)),
        scratch_shapes=[pltpu.VMEM((H_A * tq, LANES), F32), pltpu.VMEM((H_A * tq, LANES), F32),
                        pltpu.VMEM((H_A * tq, KV_RANK), F32)],
        compiler_params=_cparams(("parallel", "parallel")), name="mla_prompt",
    )(qc, kc)


def _head_rows(q):
    lane = lax.broadcasted_iota(I32, q.shape, 1)
    return jnp.concatenate([jnp.where((lane >= h * DH_B) & (lane < (h + 1) * DH_B), q, 0.0) for h in range(H_B)],
                           axis=0)


def _head_lanes(o, r):
    lane = lax.broadcasted_iota(I32, (r, o.shape[1]), 1)
    out = jnp.zeros((r, o.shape[1]), F32)
    for h in range(H_B):
        out = jnp.where((lane >= h * DH_B) & (lane < (h + 1) * DH_B), o[h * r:(h + 1) * r], out)
    return out


def _dsa_body(qih_ref, aux_ref, kib_ref, qbb_ref, kbb_ref, vbb_ref, ob_ref, sc_ref, t_ref, cut_ref, m_ref, l_ref,
              acc_ref, *, tq, ck, topk, idx_bits):
    qi = pl.program_id(1)
    nk = (qi * tq) // ck + 1
    q4 = qih_ref[0].reshape(H_IDX * tq, D_IDX)
    w_t = aux_ref[0].T[AUX_WI0:AUX_WI0 + H_IDX, :] * (H_IDX ** -0.5)
    krow = lax.broadcasted_iota(I32, (ck, tq), 0)
    qpos = qi * tq + lax.broadcasted_iota(I32, (ck, tq), 1)

    def kpos_of(c):
        return c * ck + krow

    def score_chunk(c, carry):
        kc = kib_ref[0, pl.ds(pl.multiple_of(c * ck, ck), ck), :]
        d_t = _dot_nt(kc, q4) * (D_IDX ** -0.5)
        sc = jnp.zeros((ck, tq), F32)
        for h in range(H_IDX):
            sc = sc + jnp.maximum(d_t[:, h * tq:(h + 1) * tq], 0.0) * w_t[h:h + 1, :]
        sc_ref[c] = jnp.where(kpos_of(c) <= qpos, sc, -jnp.inf)
        return carry

    lax.fori_loop(0, nk, score_chunk, 0)

    kf = jnp.float32(topk)

    def search(n):
        def count(pred):
            part = jnp.zeros((8, tq), F32)
            for c in range(n):
                part = part + jnp.sum(jnp.where(pred(c), 1.0, 0.0).reshape(ck // 8, 8, tq), axis=0)
            return jnp.sum(part, axis=0, keepdims=True)

        def value_step(i, key):
            cand = key + (jnp.int32(1) << (jnp.int32(31) - i))
            cand_f = _key_to_float(cand)
            return jnp.where(count(lambda c: sc_ref[c] >= cand_f) >= kf, cand, key)

        t = _kth_to_float(lax.fori_loop(0, 32, value_step, jnp.full((1, tq), INT_MIN, I32)))
        need = kf - count(lambda c: sc_ref[c] > t)

        def tie(c):
            return (sc_ref[c] == t) & (kpos_of(c) <= qpos)

        t_ref[0:1, :] = t
        cut_ref[0:1, :] = jnp.full((1, tq), 2 ** idx_bits, I32)
        surplus = jnp.max(count(tie) - need)

        @pl.when(surplus > 0.0)
        def _():
            def index_step(i, cut):
                cand = cut + (jnp.int32(1) << (jnp.int32(idx_bits - 1) - i))
                return jnp.where(count(lambda c: tie(c) & (kpos_of(c) < cand)) < need, cand, cut)

            cut_ref[0:1, :] = lax.fori_loop(0, idx_bits, index_step, jnp.zeros((1, tq), I32))

    t_ref[0:1, :] = jnp.full((1, tq), -jnp.inf, F32)
    cut_ref[0:1, :] = jnp.zeros((1, tq), I32)
    for n in range(1, sc_ref.shape[0] + 1):
        pl.when((nk == n) & ((qi + 1) * tq > topk))(functools.partial(search, n))
    t = t_ref[0:1, :]
    cut = cut_ref[0:1, :]

    qm = _head_rows(qbb_ref[0].astype(F32)).astype(BF16)
    _flash_init(m_ref, l_ref, acc_ref)

    def attend(c, carry):
        sc = sc_ref[c]
        sel_t = (sc > t) | ((sc == t) & (kpos_of(c) <= qpos) & (kpos_of(c) <= cut))
        keep = jnp.where(sel_t, 1.0, 0.0).T
        keep = jnp.concatenate([keep] * H_B, axis=0) > 0.5
        rows = pl.ds(pl.multiple_of(c * ck, ck), ck)
        _flash_step(qm, kbb_ref[0, rows, :], vbb_ref[0, rows, :], keep, m_ref, l_ref, acc_ref, DH_B ** -0.5)
        return carry

    lax.fori_loop(0, nk, attend, 0)
    ob_ref[0] = _head_lanes(_flash_result(l_ref, acc_ref), tq).astype(BF16)


def _dsa_call(qih, aux, kib, qbb, kbb, vbb, tq, topk):
    b, t, _ = qbb.shape
    ck = _pick_tile(t, 256)
    idx_bits = max(1, int(np.ceil(np.log2(t))))
    whole = lambda w: pl.BlockSpec((1, t, w), lambda bi, i: (bi, 0, 0))
    tile = lambda w: pl.BlockSpec((1, tq, w), lambda bi, i: (bi, i, 0))
    return pl.pallas_call(
        functools.partial(_dsa_body, tq=tq, ck=ck, topk=topk, idx_bits=idx_bits),
        out_shape=jax.ShapeDtypeStruct((b, t, B_WIDTH), BF16),
        grid=(b, t // tq),
        in_specs=[pl.BlockSpec((1, H_IDX, tq, D_IDX), lambda bi, i: (bi, 0, i, 0)), tile(LANES), whole(D_IDX),
                  tile(B_WIDTH), whole(B_WIDTH), whole(B_WIDTH)],
        out_specs=tile(B_WIDTH),
        scratch_shapes=[pltpu.VMEM((t // ck, ck, tq), F32), pltpu.VMEM((8, tq), F32), pltpu.VMEM((8, tq), I32),
                        pltpu.VMEM((H_B * tq, LANES), F32),
                        pltpu.VMEM((H_B * tq, LANES), F32), pltpu.VMEM((H_B * tq, B_WIDTH), F32)],
        compiler_params=_cparams(("parallel", "parallel")), name="dsa_prompt",
    )(qih, aux, kib, qbb, kbb, vbb)


def _pool_delta(win_sums, cnts, xcur):
    lane = lax.broadcasted_iota(I32, xcur.shape, 1)
    mean = jnp.zeros(xcur.shape, F32)
    for g, w in enumerate(POOL_WINDOWS):
        in_group = (lane >= g * POOL_CH) & (lane < (g + 1) * POOL_CH)
        mean = jnp.where(in_group, win_sums[w] / cnts[w], mean)
    return mean - xcur


def _merge_tail(o_lat_heads, ob, delta, x, wuv_ref, woa_ref, wob_ref, woc_ref, wp_ref, cs_ref, g_ref, b_ref):
    oc = _dot(delta.astype(BF16), wp_ref[...]) * cs_ref[...]
    acc = _dot(ob, wob_ref[...]) + _dot(oc.astype(BF16), woc_ref[...])
    for h in range(H_A):
        oa = _dot(o_lat_heads(h), wuv_ref[h]).astype(BF16)
        acc = acc + _dot(oa, woa_ref[h])
    return _layer_norm(ALPHA * x + acc, g_ref[...], b_ref[...])


def _merge_body(ol_ref, ob_ref, xc_ref, buf_ref, x_ref, wuv_ref, woa_ref, wob_ref, woc_ref, wp_ref, cs_ref, g_ref,
                b_ref, h_ref, ext_ref, *, tm):
    i = pl.program_id(1)
    start = pl.multiple_of(i * tm, tm)

    @pl.when(i == 0)
    def _():
        ext_ref[0:HALO, :] = buf_ref[0]

    @pl.when(i > 0)
    def _():
        ext_ref[0:HALO, :] = xc_ref[0, pl.ds(start - HALO, HALO), :]

    xcur = xc_ref[0, pl.ds(start, tm), :]
    ext_ref[HALO:HALO + tm, :] = xcur
    run = xcur
    win_sums = {}
    for k in range(1, max(POOL_WINDOWS)):
        run = run + ext_ref[HALO - k:HALO - k + tm, :]
        if k + 1 in POOL_WINDOWS:
            win_sums[k + 1] = run
    pos1 = (start + 1 + lax.broadcasted_iota(I32, (tm, 1), 0)).astype(F32)
    cnts = {w: jnp.minimum(jnp.float32(w), pos1) for w in POOL_WINDOWS}
    delta = _pool_delta(win_sums, cnts, xcur)
    h_ref[0] = _merge_tail(lambda h: ol_ref[0, h], ob_ref[0], delta, x_ref[0], wuv_ref, woa_ref, wob_ref, woc_ref,
                           wp_ref, cs_ref, g_ref, b_ref)


def _merge_call(o_lat, ob, xc, buf, x, mw, tm):
    b, t, d = x.shape
    tile = lambda w: pl.BlockSpec((1, tm, w), lambda bi, i: (bi, i, 0))
    full = lambda a: pl.BlockSpec(a.shape, lambda bi, i: (0,) * a.ndim)
    wnames = ('wuv', 'woa', 'wob', 'woc', 'wp', 'cs', 'g1', 'b1')
    return pl.pallas_call(
        functools.partial(_merge_body, tm=tm),
        out_shape=jax.ShapeDtypeStruct((b, t, d), F32),
        grid=(b, t // tm),
        in_specs=[pl.BlockSpec((1, H_A, tm, KV_RANK), lambda bi, i: (bi, 0, i, 0)), tile(B_WIDTH),
                  pl.BlockSpec((1, t, C_WIDTH), lambda bi, i: (bi, 0, 0)),
                  pl.BlockSpec((1, HALO, C_WIDTH), lambda bi, i: (bi, 0, 0)), tile(d)]
                 + [full(mw[n]) for n in wnames],
        out_specs=tile(d),
        scratch_shapes=[pltpu.VMEM((HALO + tm, C_WIDTH), F32)],
        compiler_params=_cparams(("parallel", "arbitrary")), name="merge_prompt",
    )(o_lat, ob, xc, buf, x, *[mw[n] for n in wnames])


def _merge_sample_body(ol_ref, ob_ref, ext_ref, x_ref, wuv_ref, woa_ref, wob_ref, woc_ref, wp_ref, cs_ref, g_ref,
                       b_ref, h_ref, *, n_seen):
    n = ext_ref.shape[1]
    xcur = ext_ref[HALO - 1]
    run = xcur
    win_sums = {}
    for k in range(1, max(POOL_WINDOWS)):
        run = run + ext_ref[HALO - 1 - k]
        if k + 1 in POOL_WINDOWS:
            win_sums[k + 1] = run
    cnts = {w: jnp.full((n, 1), min(w, n_seen), F32) for w in POOL_WINDOWS}
    delta = _pool_delta(win_sums, cnts, xcur)
    h_ref[...] = _merge_tail(lambda h: ol_ref[h], ob_ref[...], delta, x_ref[...], wuv_ref, woa_ref, wob_ref, woc_ref,
                             wp_ref, cs_ref, g_ref, b_ref)


def _merge_sample_call(o_lat_h, ob, ext, x, mw, n_seen):
    n, d = x.shape
    wnames = ('wuv', 'woa', 'wob', 'woc', 'wp', 'cs', 'g1', 'b1')
    return pl.pallas_call(
        functools.partial(_merge_sample_body, n_seen=n_seen),
        out_shape=jax.ShapeDtypeStruct((n, d), F32), name="merge_sample",
        compiler_params=pltpu.CompilerParams(vmem_limit_bytes=VMEM_LIMIT),
    )(o_lat_h, ob, ext, x, *[mw[n_] for n_ in wnames])


def _swiglu_partial(xb, wg, wu, wd):
    a = _dot(xb, wg)
    u = _dot(xb, wu)
    hm = (a / (1.0 + jnp.exp(-a))) * u
    return _dot(hm.astype(BF16), wd)


def _ffn_body(h_ref, wg_ref, wu_ref, wd_ref, g_ref, b_ref, o_ref, xb_ref, acc_ref):
    k = pl.program_id(1)

    @pl.when(k == 0)
    def _():
        xb_ref[...] = h_ref[...].astype(BF16)
        acc_ref[...] = jnp.zeros(acc_ref.shape, F32)

    acc_ref[...] += _swiglu_partial(xb_ref[...], wg_ref[...], wu_ref[...], wd_ref[...])

    @pl.when(k == pl.num_programs(1) - 1)
    def _():
        o_ref[...] = _layer_norm(ALPHA * h_ref[...] + acc_ref[...], g_ref[...], b_ref[...])


def _ffn_call(h, wg, wu, wd, g, bta, tm, tf):
    n, d = h.shape
    f = wg.shape[1]
    return pl.pallas_call(
        _ffn_body, out_shape=jax.ShapeDtypeStruct((n, d), F32), grid=(n // tm, f // tf),
        in_specs=[pl.BlockSpec((tm, d), lambda i, k: (i, 0)), pl.BlockSpec((d, tf), lambda i, k: (0, k)),
                  pl.BlockSpec((d, tf), lambda i, k: (0, k)), pl.BlockSpec((tf, d), lambda i, k: (k, 0)),
                  pl.BlockSpec((1, d), lambda i, k: (0, 0)), pl.BlockSpec((1, d), lambda i, k: (0, 0))],
        out_specs=pl.BlockSpec((tm, d), lambda i, k: (i, 0)),
        scratch_shapes=[pltpu.VMEM((tm, d), BF16), pltpu.VMEM((tm, d), F32)],
        compiler_params=_cparams(("parallel", "arbitrary")), name="ffn_dense",
    )(h, wg, wu, wd, g, bta)


def _router_body(h_ref, r_ref, gate_ref):
    logits = jnp.dot(h_ref[...], r_ref[...], precision=lax.Precision.HIGHEST, preferred_element_type=F32)
    lane = lax.broadcasted_iota(I32, logits.shape, 1).astype(F32)
    logits = jnp.where(lane < N_EXPERTS, logits, -jnp.inf)
    v1 = jnp.max(logits, axis=-1, keepdims=True)
    i1 = jnp.min(jnp.where(logits == v1, lane, float(LANES)), axis=-1, keepdims=True)
    rest = jnp.where(lane == i1, -jnp.inf, logits)
    v2 = jnp.max(rest, axis=-1, keepdims=True)
    i2 = jnp.min(jnp.where(rest == v2, lane, float(LANES)), axis=-1, keepdims=True)
    e2 = jnp.exp(v2 - v1)
    g1 = 1.0 / (1.0 + e2)
    g2 = e2 / (1.0 + e2)
    dense = jnp.where(lane == i1, g1, jnp.where(lane == i2, g2, 0.0))
    for off, val in enumerate((i1, i2, g1, g2)):
        dense = jnp.where(lane == float(ROUTE_LANE0 + off), val, dense)
    gate_ref[...] = dense


def _router_call(h, router_pad, tm):
    n, d = h.shape
    return pl.pallas_call(
        _router_body, out_shape=jax.ShapeDtypeStruct((n, LANES), F32), grid=(n // tm,),
        in_specs=[pl.BlockSpec((tm, d), lambda i: (i, 0)), pl.BlockSpec((d, LANES), lambda i: (0, 0))],
        out_specs=pl.BlockSpec((tm, LANES), lambda i: (i, 0)),
        compiler_params=_cparams(("parallel",)), name="moe_router",
    )(h, router_pad)


def _moe_body(h_ref, gate_ref, wg_ref, wu_ref, wd_ref, g_ref, b_ref, o_ref, xb_ref, acc_ref):
    e = pl.program_id(1)
    k = pl.program_id(2)

    @pl.when((e == 0) & (k == 0))
    def _():
        xb_ref[...] = h_ref[...].astype(BF16)
        acc_ref[...] = jnp.zeros(acc_ref.shape, F32)

    gates = gate_ref[...]
    lane = lax.broadcasted_iota(I32, gates.shape, 1)
    ge = jnp.sum(jnp.where(lane == e, gates, 0.0), axis=-1, keepdims=True)
    acc_ref[...] += ge * _swiglu_partial(xb_ref[...], wg_ref[0], wu_ref[0], wd_ref[0])

    @pl.when((e == pl.num_programs(1) - 1) & (k == pl.num_programs(2) - 1))
    def _():
        o_ref[...] = _layer_norm(ALPHA * h_ref[...] + acc_ref[...], g_ref[...], b_ref[...])


def _moe_call(h, gates, wg, wu, wd, g, bta, tm, tf):
    n, d = h.shape
    ne, _, f = wg.shape
    return pl.pallas_call(
        _moe_body, out_shape=jax.ShapeDtypeStruct((n, d), F32), grid=(n // tm, ne, f // tf),
        in_specs=[pl.BlockSpec((tm, d), lambda i, e, k: (i, 0)), pl.BlockSpec((tm, LANES), lambda i, e, k: (i, 0)),
                  pl.BlockSpec((1, d, tf), lambda i, e, k: (e, 0, k)),
                  pl.BlockSpec((1, d, tf), lambda i, e, k: (e, 0, k)),
                  pl.BlockSpec((1, tf, d), lambda i, e, k: (e, k, 0)),
                  pl.BlockSpec((1, d), lambda i, e, k: (0, 0)), pl.BlockSpec((1, d), lambda i, e, k: (0, 0))],
        out_specs=pl.BlockSpec((tm, d), lambda i, e, k: (i, 0)),
        scratch_shapes=[pltpu.VMEM((tm, d), BF16), pltpu.VMEM((tm, d), F32)],
        compiler_params=_cparams(("parallel", "arbitrary", "arbitrary")), name="ffn_moe",
    )(h, gates, wg, wu, wd, g, bta)


def _route_plan(route, tm):
    n = route.shape[0]
    ea = route[:, ROUTE_LANE0:ROUTE_LANE0 + 2].astype(I32).T.reshape(-1)
    ga = route[:, ROUTE_LANE0 + 2:ROUTE_LANE0 + 4].T.reshape(-1)
    onehot = (ea[:, None] == jnp.arange(N_EXPERTS, dtype=I32)[None, :]).astype(I32)
    csum = jnp.cumsum(onehot, axis=0)
    counts = csum[-1]
    rank = jnp.sum(onehot * csum, axis=1) - 1
    tiles_e = (counts + tm - 1) // tm
    tile_end = jnp.cumsum(tiles_e)
    tile_off = tile_end - tiles_e
    pos = tile_off[ea] * tm + rank
    n_tiles = (2 * n) // tm + N_EXPERTS
    pair = jnp.arange(2 * n, dtype=I32)
    pair_of = jnp.full((n_tiles * tm,), -1, I32).at[pos].set(pair)
    real = pair_of >= 0
    tok_of = jnp.where(real, pair_of % n, 0)
    slot_row = jnp.arange(n_tiles * tm, dtype=I32)
    spare = 2 * n + ((slot_row // tm) % 2) * tm + slot_row % tm
    dst_of = jnp.where(real, pair_of, spare)
    gate_of = jnp.where(real, ga[jnp.maximum(pair_of, 0)], 0.0)
    tile_ids = jnp.arange(n_tiles, dtype=I32)
    te = jnp.minimum(jnp.sum((tile_ids[:, None] >= tile_end[None, :]).astype(I32), axis=1), N_EXPERTS - 1)
    nv = jnp.clip(counts[te] - (tile_ids - tile_off[te]) * tm, 0, tm)
    return (tok_of.reshape(n_tiles, 1, tm), dst_of.reshape(n_tiles, 1, tm), gate_of.reshape(n_tiles * tm, 1),
            te.astype(I32), nv.astype(I32))


def _moe_routed_body(te_ref, nv_ref, tokc_ref, tokn_ref, dstp_ref, dstc_ref, gate_ref, h_hbm, wg_ref, wu_ref, wd_ref,
                     y_hbm, xbuf, xb_ref, acc_ref, ybuf, gsem, ssem, *, tm, rows_per_step):
    i = pl.program_id(0)
    k = pl.program_id(1)
    n_t = pl.num_programs(0)
    slot = i % 2
    other = 1 - slot

    def gather_copy(tok_ref, r, buf_slot):
        return pltpu.make_async_copy(h_hbm.at[pl.ds(tok_ref[0, 0, r], 1), :], xbuf.at[buf_slot, pl.ds(r, 1), :],
                                     gsem.at[buf_slot])

    def scatter_copy(dst_ref, r, buf_slot):
        return pltpu.make_async_copy(ybuf.at[buf_slot, pl.ds(r, 1), :], y_hbm.at[pl.ds(dst_ref[0, 0, r], 1), :],
                                     ssem.at[buf_slot])

    def start_rows(copy_of):
        def body(r, carry):
            copy_of(r).start()
            return carry
        lax.fori_loop(0, tm, body, 0, unroll=8)

    def wait_gather(buf_slot):
        pltpu.make_async_copy(h_hbm.at[pl.ds(0, tm), :], xbuf.at[buf_slot], gsem.at[buf_slot]).wait()

    def wait_scatter(buf_slot):
        pltpu.make_async_copy(ybuf.at[buf_slot], y_hbm.at[pl.ds(0, tm), :], ssem.at[buf_slot]).wait()

    @pl.when((i == 0) & (k == 0))
    def _():
        start_rows(lambda r: gather_copy(tokc_ref, r, 0))
        ybuf[1] = jnp.zeros(ybuf.shape[1:], F32)

    @pl.when(k == 0)
    def _():
        wait_gather(slot)
        xb_ref[...] = xbuf[slot].astype(BF16)
        acc_ref[...] = jnp.zeros(acc_ref.shape, F32)

    base = k * rows_per_step
    for j in range(rows_per_step):
        gather_copy(tokn_ref, base + j, other).start()
        scatter_copy(dstp_ref, base + j, other).start()

    @pl.when(nv_ref[i] > 0)
    def _():
        acc_ref[...] += _swiglu_partial(xb_ref[...], wg_ref[0], wu_ref[0], wd_ref[0])

    @pl.when(k == pl.num_programs(1) - 1)
    def _():
        @pl.when(i > 0)
        def _():
            wait_scatter(slot)

        ybuf[slot] = acc_ref[...] * gate_ref[...]

        @pl.when(i == n_t - 1)
        def _():
            start_rows(lambda r: scatter_copy(dstc_ref, r, slot))
            wait_scatter(other)
            wait_scatter(slot)
            wait_gather(other)


def _moe_routed_call(h, route, wg, wu, wd, tm, tf):
    n, d = h.shape
    f = wg.shape[2]
    n_k = f // tf
    assert f % tf == 0 and tm % n_k == 0, (f, tf, tm)
    tok_of, dst_of, gate_of, te, nv = _route_plan(route, tm)
    n_tiles = tok_of.shape[0]
    first_prev = (2 * n + 2 * tm + jnp.arange(tm, dtype=I32)).reshape(1, 1, tm)
    dst_prev = jnp.concatenate([first_prev, dst_of[:-1]], axis=0)
    smem_row = lambda fn: pl.BlockSpec((1, 1, tm), fn, memory_space=pltpu.SMEM)
    cur = lambda i, k, te_, nv_: (i, 0, 0)
    grid_spec = pltpu.PrefetchScalarGridSpec(
        num_scalar_prefetch=2, grid=(n_tiles, n_k),
        in_specs=[smem_row(cur),
                  smem_row(lambda i, k, te_, nv_: (jnp.minimum(i + 1, n_tiles - 1), 0, 0)),
                  smem_row(cur), smem_row(cur),
                  pl.BlockSpec((tm, 1), lambda i, k, te_, nv_: (i, 0)),
                  pl.BlockSpec(memory_space=pl.ANY),
                  pl.BlockSpec((1, d, tf), lambda i, k, te_, nv_: (te_[i], 0, k)),
                  pl.BlockSpec((1, d, tf), lambda i, k, te_, nv_: (te_[i], 0, k)),
                  pl.BlockSpec((1, tf, d), lambda i, k, te_, nv_: (te_[i], k, 0))],
        out_specs=pl.BlockSpec(memory_space=pl.ANY),
        scratch_shapes=[pltpu.VMEM((2, tm, d), F32), pltpu.VMEM((tm, d), BF16), pltpu.VMEM((tm, d), F32),
                        pltpu.VMEM((2, tm, d), F32), pltpu.SemaphoreType.DMA((2,)), pltpu.SemaphoreType.DMA((2,))])
    return pl.pallas_call(
        functools.partial(_moe_routed_body, tm=tm, rows_per_step=tm // n_k),
        out_shape=jax.ShapeDtypeStruct((2 * n + 3 * tm, d), F32), grid_spec=grid_spec,
        compiler_params=_cparams(("arbitrary", "arbitrary")), name="ffn_moe_routed",
    )(te, nv, tok_of, tok_of, dst_prev, dst_of, gate_of, h, wg, wu, wd)


def _combine_body(h_ref, y0_ref, y1_ref, g_ref, b_ref, o_ref):
    o_ref[...] = _layer_norm(ALPHA * h_ref[...] + (y0_ref[...] + y1_ref[...]), g_ref[...], b_ref[...])


def _combine_call(h, y2, g, bta, tm):
    n, d = h.shape
    return pl.pallas_call(
        _combine_body, out_shape=jax.ShapeDtypeStruct((n, d), F32), grid=(n // tm,),
        in_specs=[pl.BlockSpec((tm, d), lambda i: (i, 0)), pl.BlockSpec((tm, d), lambda i: (i, 0)),
                  pl.BlockSpec((tm, d), lambda i: (i + n // tm, 0)),
                  pl.BlockSpec((1, d), lambda i: (0, 0)), pl.BlockSpec((1, d), lambda i: (0, 0))],
        out_specs=pl.BlockSpec((tm, d), lambda i: (i, 0)),
        compiler_params=_cparams(("parallel",)), name="moe_combine",
    )(h, y2, y2, g, bta)


def _stream_page_groups(pt_ref, layer, group, streams, sems, compute):
    b = pl.program_id(0)
    n_seq = pl.num_programs(0)
    n_groups = pt_ref.shape[1] // group
    ahead = STREAM_SLOTS - 1

    def start(seq, g, slot):
        for j in range(group):
            page = pt_ref[seq, g * group + j]
            for a, (hbm, buf) in enumerate(streams):
                pltpu.make_async_copy(hbm.at[layer, page], buf.at[slot, j], sems.at[a, slot]).start()

    def wait(slot):
        for a, (hbm, buf) in enumerate(streams):
            pltpu.make_async_copy(hbm.at[layer, pl.ds(0, group)], buf.at[slot], sems.at[a, slot]).wait()

    @pl.when(b == 0)
    def _():
        for g0 in range(ahead):
            start(0, g0, g0)

    def body(g, carry):
        slot = g % STREAM_SLOTS
        wait(slot)
        nxt = g + ahead
        nxt_slot = nxt % STREAM_SLOTS

        @pl.when(nxt < n_groups)
        def _():
            start(b, nxt, nxt_slot)

        @pl.when((nxt >= n_groups) & (b + 1 < n_seq))
        def _():
            start(b + 1, nxt - n_groups, nxt_slot)

        compute(g, slot)
        return carry

    lax.fori_loop(0, n_groups, body, 0)


def _mla_idx_sample_body(pt_ref, q_ref, knew_ref, qi_ref, w_ref, ck_hbm, kr_hbm, ki_hbm, o_ref, sc_ref,
                         ckbuf, krbuf, kibuf, sems, m_ref, l_ref, acc_ref, *, layer, group):
    _flash_init(m_ref, l_ref, acc_ref)
    q = q_ref[0]
    q_rope = q[:, KV_RANK + ROPE_LANE0:KV_RANK + ROPE_LANE0 + ROPE_A]
    w = w_ref[0] * (H_IDX ** -0.5)

    def compute(g, slot):
        ck = ckbuf[slot].reshape(group * PAGE_SIZE, KV_RANK).astype(BF16)
        kr_t = jnp.concatenate([krbuf[slot, j] for j in range(group)], axis=1).astype(BF16)
        s = (_dot_nt(q[:, :KV_RANK], ck) + _dot(q_rope, kr_t)) * (MLA_SCALE * LOG2E)
        _flash_update(s, ck, m_ref, l_ref, acc_ref)
        ki_t = jnp.concatenate([kibuf[slot, j] for j in range(group)], axis=1).astype(BF16)
        dots = _dot(qi_ref[0], ki_t) * (D_IDX ** -0.5)
        sc_ref[0, pl.ds(g, 1), :] = jnp.sum(jnp.maximum(dots, 0.0) * w, axis=0, keepdims=True)

    _stream_page_groups(pt_ref, layer, group, ((ck_hbm, ckbuf), (kr_hbm, krbuf), (ki_hbm, kibuf)), sems, compute)
    kn = knew_ref[0].astype(F32)
    s_n = jnp.sum(q.astype(F32) * kn, axis=-1, keepdims=True) * (MLA_SCALE * LOG2E)
    m_old = m_ref[...]
    m_fin = jnp.maximum(m_old, s_n)
    a = jnp.exp2(m_old - m_fin)
    p_n = jnp.exp2(s_n - m_fin)[:, 0:1]
    l_fin = jnp.sum(a * l_ref[...], axis=-1, keepdims=True) + p_n
    acc = _lane_tile(a, KV_RANK) * acc_ref[...] + p_n * kn[:, :KV_RANK]
    o_ref[0] = (acc * (1.0 / l_fin)).astype(BF16)


def _page_group(n_pages, most):
    g = _pick_tile(n_pages, most)
    while g > 1 and (n_pages // g) % STREAM_SLOTS:
        g //= 2
    assert (n_pages // g) % STREAM_SLOTS == 0, n_pages
    return g


def _mla_idx_sample_call(page_table, q, knew, qi, wi, cache_ckv, cache_kr_t, cache_ki_t, layer):
    n, n_pages = page_table.shape
    group = _page_group(n_pages, 16)
    n_groups = n_pages // group
    per_seq = lambda *blk: pl.BlockSpec((1,) + blk, lambda bi, pt: (bi,) + (0,) * len(blk))
    any_spec = pl.BlockSpec(memory_space=pl.ANY)
    grid_spec = pltpu.PrefetchScalarGridSpec(
        num_scalar_prefetch=1, grid=(n,),
        in_specs=[per_seq(H_A, QC_W), per_seq(1, QC_W), per_seq(H_IDX, D_IDX), per_seq(H_IDX, 1),
                  any_spec, any_spec, any_spec],
        out_specs=(per_seq(H_A, KV_RANK), per_seq(n_groups, group * PAGE_SIZE)),
        scratch_shapes=[pltpu.VMEM((STREAM_SLOTS, group, PAGE_SIZE, KV_RANK), F32),
                        pltpu.VMEM((STREAM_SLOTS, group, ROPE_A, PAGE_SIZE), F32),
                        pltpu.VMEM((STREAM_SLOTS, group, D_IDX, PAGE_SIZE), F32),
                        pltpu.SemaphoreType.DMA((3, STREAM_SLOTS)),
                        pltpu.VMEM((H_A, LANES), F32), pltpu.VMEM((H_A, LANES), F32),
                        pltpu.VMEM((H_A, KV_RANK), F32)])
    o_lat, score = pl.pallas_call(
        functools.partial(_mla_idx_sample_body, layer=layer, group=group),
        out_shape=(jax.ShapeDtypeStruct((n, H_A, KV_RANK), BF16),
                   jax.ShapeDtypeStruct((n, n_groups, group * PAGE_SIZE), F32)),
        grid_spec=grid_spec, compiler_params=_cparams(("arbitrary",)), name="mla_idx_sample",
    )(page_table, q, knew, qi, wi, cache_ckv, cache_kr_t, cache_ki_t)
    return o_lat, score.reshape(n, n_pages * PAGE_SIZE)


def _select_sample_body(sp_ref, qi_ref, w_ref, kin_ref, sel_ref, *, past, topk, idx_bits):
    n = sp_ref.shape[0]
    kn = kin_ref[...].astype(F32)
    s_new = jnp.zeros((n, 1), F32)
    for h in range(H_IDX):
        d = jnp.sum(qi_ref[h].astype(F32) * kn, axis=-1, keepdims=True) * (D_IDX ** -0.5)
        s_new = s_new + jnp.maximum(d, 0.0) * (w_ref[:, h:h + 1] * (H_IDX ** -0.5))
    lane = lax.broadcasted_iota(I32, (n, LANES), 1)
    tail = jnp.where(lane == 0, s_new, 0.0)
    score = jnp.concatenate([sp_ref[...], tail], axis=-1)
    kpos = lax.broadcasted_iota(I32, score.shape, 1)
    sel_ref[...] = jnp.where(_topk_mask(score, kpos <= past, kpos, topk, idx_bits), 1.0, 0.0)


def _select_sample_call(score_past, qi, wi, ki_new, topk):
    n, past = score_past.shape
    idx_bits = int(np.ceil(np.log2(past + LANES)))
    return pl.pallas_call(
        functools.partial(_select_sample_body, past=past, topk=topk, idx_bits=idx_bits),
        out_shape=jax.ShapeDtypeStruct((n, past + LANES), F32),
        compiler_params=pltpu.CompilerParams(vmem_limit_bytes=VMEM_LIMIT), name="select_sample",
    )(score_past, qi, wi, ki_new)


def _dsa_sample_body(pt_ref, q_ref, sel_ref, seln_ref, kn_ref, vn_ref, k_hbm, v_hbm, o_ref, kbuf, vbuf, sems,
                     m_ref, l_ref, acc_ref, *, layer, group):
    _flash_init(m_ref, l_ref, acc_ref)
    qcol = q_ref[0]
    scale = (DH_B ** -0.5) * LOG2E

    def compute(g, slot):
        keep = sel_ref[0, pl.ds(g, 1), :] > 0.5
        for h in range(H_B):
            qh = qcol[h * DH_B:(h + 1) * DH_B]
            k_t = jnp.concatenate([kbuf[slot, j, h] for j in range(group)], axis=1)
            s = jnp.sum(k_t * qh, axis=0, keepdims=True) * scale
            s = jnp.where(keep, s, -jnp.inf)
            m_prev = m_ref[h]
            m_new = jnp.maximum(m_prev, jnp.max(s, axis=-1, keepdims=True))
            m_use = jnp.where(m_new == -jnp.inf, 0.0, m_new)
            alpha = jnp.exp2(m_prev - m_use)
            p = jnp.exp2(s - _lane_tile(m_use, s.shape[1]))
            v_t = jnp.concatenate([vbuf[slot, j, h] for j in range(group)], axis=1)
            l_ref[h] = alpha * l_ref[h] + _lane_fold(p)
            acc_ref[h] = alpha * acc_ref[h] + _lane_fold(v_t * p)
            m_ref[h] = m_new

    _stream_page_groups(pt_ref, layer, group, ((k_hbm, kbuf), (v_hbm, vbuf)), sems, compute)
    new_kept = seln_ref[0][:, 0:1] > 0.5
    for h in range(H_B):
        rows = slice(h * DH_B, (h + 1) * DH_B)
        s_n = jnp.sum(kn_ref[0][rows] * qcol[rows], axis=0, keepdims=True) * scale
        s_n = jnp.where(new_kept, s_n, -jnp.inf)
        m_prev = m_ref[h]
        m_new = jnp.maximum(m_prev, s_n)
        m_use = jnp.where(m_new == -jnp.inf, 0.0, m_new)
        alpha = jnp.exp2(m_prev - m_use)
        p_n = jnp.exp2(s_n - m_use)[:, 0:1]
        l_tot = jnp.sum(alpha * l_ref[h], axis=-1, keepdims=True) + p_n
        acc = jnp.sum(alpha * acc_ref[h], axis=-1, keepdims=True) + p_n * vn_ref[0][rows]
        o_ref[0, rows, :] = acc * (1.0 / l_tot)


def _dsa_sample_call(page_table, q_col, sel, kn_col, vn_col, cache_k_t, cache_v_t, layer):
    n, n_pages = page_table.shape
    group = _page_group(n_pages, 8)
    n_groups = n_pages // group
    past = n_pages * PAGE_SIZE
    sel_past = sel[:, :past].reshape(n, n_groups, group * PAGE_SIZE)
    sel_new = sel[:, past:].reshape(n, 1, LANES)
    per_seq = lambda *blk: pl.BlockSpec((1,) + blk, lambda bi, pt: (bi,) + (0,) * len(blk))
    any_spec = pl.BlockSpec(memory_space=pl.ANY)
    grid_spec = pltpu.PrefetchScalarGridSpec(
        num_scalar_prefetch=1, grid=(n,),
        in_specs=[per_seq(B_WIDTH, 1), per_seq(n_groups, group * PAGE_SIZE), per_seq(1, LANES), per_seq(B_WIDTH, 1),
                  per_seq(B_WIDTH, 1), any_spec, any_spec],
        out_specs=per_seq(B_WIDTH, 1),
        scratch_shapes=[pltpu.VMEM((STREAM_SLOTS, group, H_B, DH_B, PAGE_SIZE), F32),
                        pltpu.VMEM((STREAM_SLOTS, group, H_B, DH_B, PAGE_SIZE), F32),
                        pltpu.SemaphoreType.DMA((2, STREAM_SLOTS)),
                        pltpu.VMEM((H_B, 1, LANES), F32), pltpu.VMEM((H_B, 1, LANES), F32),
                        pltpu.VMEM((H_B, DH_B, LANES), F32)])
    return pl.pallas_call(
        functools.partial(_dsa_sample_body, layer=layer, group=group),
        out_shape=jax.ShapeDtypeStruct((n, B_WIDTH, 1), F32), grid_spec=grid_spec,
        compiler_params=_cparams(("arbitrary",)), name="dsa_sample",
    )(page_table, q_col, sel_past, sel_new, kn_col, vn_col, cache_k_t, cache_v_t)


def _rot_cols(w, head_dim):
    k, n = w.shape
    wh = w.reshape(k, n // head_dim, 2, head_dim // 2)
    return jnp.concatenate([-wh[:, :, 1], wh[:, :, 0]], axis=-1).reshape(k, n)


def _prep_layer(w_in, a_q_norm, a_kv_norm, a_w_uq, a_w_uk, a_w_uv, c_w_pool, c_scale, w_out, ln_g, ln_b):
    offs = np.cumsum((0,) + IN_SPLITS)
    cq, ckv, kr, qb, kb, vb, qi, ki, wi, xc = [w_in[:, offs[i]:offs[i + 1]] for i in range(len(IN_SPLITS))]
    d = w_in.shape[0]
    zeros = lambda n: jnp.zeros((d, n), F32)
    grp_a = jnp.concatenate([ki, kr, wi, zeros(LANES - D_IDX - ROPE_A - H_IDX)], axis=1)
    grp_ar = jnp.concatenate([_rot_cols(ki, D_IDX), _rot_cols(kr, ROPE_A), zeros(LANES - D_IDX - ROPE_A)], axis=1)
    w1 = jnp.concatenate([cq, ckv, qb, _rot_cols(qb, DH_B), kb, _rot_cols(kb, DH_B), vb, qi, _rot_cols(qi, D_IDX),
                          xc, grp_a, grp_ar], axis=1).astype(BF16)
    uq = a_w_uq.reshape(Q_RANK, H_A, NOPE_A + ROPE_A)
    nope = jnp.pad(uq[:, :, :NOPE_A], ((0, 0), (0, 0), (0, LANES - NOPE_A)))
    rope = uq[:, :, NOPE_A:]
    rope_rot = jnp.concatenate([-rope[..., ROPE_A // 2:], rope[..., :ROPE_A // 2]], axis=-1)
    pad_rope = lambda r: jnp.pad(r, ((0, 0), (0, 0), (ROPE_LANE0, LANES - ROPE_LANE0 - ROPE_A)))
    wq = jnp.concatenate([nope.reshape(Q_RANK, -1), pad_rope(rope).reshape(Q_RANK, -1),
                          pad_rope(rope_rot).reshape(Q_RANK, -1)], axis=1).astype(BF16)
    wuk = jnp.pad(jnp.transpose(a_w_uk, (1, 2, 0)), ((0, 0), (0, LANES - NOPE_A), (0, 0))).astype(BF16)
    proj_w = dict(w1=w1, wq=wq, wuk=wuk, gq=a_q_norm.reshape(1, -1), gkv=a_kv_norm.reshape(1, -1))
    wuv = jnp.pad(jnp.transpose(a_w_uv, (1, 0, 2)), ((0, 0), (0, 0), (0, LANES - V_A))).astype(BF16)
    woa = jnp.pad(w_out[:A_WIDTH].reshape(H_A, V_A, -1), ((0, 0), (0, LANES - V_A), (0, 0))).astype(BF16)
    wp = jnp.zeros((C_WIDTH, C_WIDTH), F32)
    for g in range(len(POOL_WINDOWS)):
        wp = wp.at[g * POOL_CH:(g + 1) * POOL_CH, g * POOL_CH:(g + 1) * POOL_CH].set(c_w_pool[g])
    merge_w = dict(wuv=wuv, woa=woa, wob=w_out[A_WIDTH:A_WIDTH + B_WIDTH].astype(BF16),
                   woc=w_out[A_WIDTH + B_WIDTH:].astype(BF16), wp=wp.astype(BF16), cs=c_scale.reshape(1, -1),
                   g1=ln_g.reshape(1, -1), b1=ln_b.reshape(1, -1))
    return proj_w, merge_w


def _rope_tables(pos):
    posf = pos.astype(F32)[:, None]

    def cs(dim):
        inv = ROPE_THETA ** (-jnp.arange(0, dim, 2, dtype=F32) / dim)
        ang = posf * inv[None, :]
        c, s = jnp.cos(ang), jnp.sin(ang)
        return jnp.concatenate([c, c], -1), jnp.concatenate([s, s], -1)

    c64, s64 = cs(DH_B)
    c32, s32 = cs(ROPE_A)
    t = pos.shape[0]
    z = lambda n: jnp.zeros((t, n), F32)
    c_pair, s_pair = jnp.concatenate([c64, c64], -1), jnp.concatenate([s64, s64], -1)
    c_a = jnp.concatenate([c64, c32, jnp.ones((t, H_IDX), F32), z(LANES - D_IDX - ROPE_A - H_IDX)], -1)
    s_a = jnp.concatenate([s64, s32, z(LANES - D_IDX - ROPE_A)], -1)
    c_q = jnp.concatenate([z(ROPE_LANE0), c32, z(LANES - ROPE_LANE0 - ROPE_A)], -1)
    s_q = jnp.concatenate([z(ROPE_LANE0), s32, z(LANES - ROPE_LANE0 - ROPE_A)], -1)
    return c_pair, s_pair, c_a, s_a, c_q, s_q


def _aux_split(aux):
    return aux[..., D_IDX:D_IDX + ROPE_A], aux[..., :D_IDX]


def _mixers_prompt(x, proj_w, merge_w, tabs, tiles):
    b, t, _ = x.shape
    (qc, kc, ckv_n, kb, vb, aux, kbb, vbb, kib, qbb, qih, xc) = _proj_call(x, proj_w, tabs, tiles['proj'])
    o_lat = _mla_call(qc, kc, tiles['mla'])
    ob = _dsa_call(qih, aux, kib, qbb, kbb, vbb, tiles['dsa'], min(TOPK_MAX, t // 4))
    buf = jnp.zeros((b, HALO, C_WIDTH), F32)
    h = _merge_call(o_lat, ob, xc, buf, x, merge_w, tiles['merge'])
    kr_r, ki_r = _aux_split(aux)
    state = (ckv_n, kr_r, kb.reshape(b, t, H_B, DH_B), vb.reshape(b, t, H_B, DH_B), ki_r, xc[:, t - POOL_BUF:])
    return h, state


def _mixers_sample(x, proj_w, merge_w, tabs, caches, state_pool, page_table, layer):
    n = x.shape[0]
    c_ckv, c_kr, c_k, c_v, c_kidx = caches
    (qc, kc, ckv_n, kb, vb, aux, kbb, vbb, kib, qbb, qih, xc) = _proj_call(x.reshape(1, n, -1), proj_w, tabs, n)
    q = jnp.transpose(qc[0], (1, 0, 2))
    qi = jnp.transpose(qih[0], (1, 0, 2))
    wi = aux[0][:, AUX_WI0:AUX_WI0 + H_IDX]
    o_lat, score_past = _mla_idx_sample_call(page_table, q, kc[0][:, None, :], qi, wi[:, :, None], c_ckv, c_kr,
                                             c_kidx, layer)
    past = score_past.shape[1]
    sel = _select_sample_call(score_past, qih[0], wi, kib[0], min(TOPK_MAX, (past + 1) // 4))
    col = lambda a: a.astype(F32)[:, :, None]
    ob = _dsa_sample_call(page_table, col(qbb[0]), sel, col(kb[0]), col(vb[0]), c_k, c_v,
                          layer)[:, :, 0].astype(BF16)
    ext = jnp.concatenate([state_pool, xc[0][:, None, :]], axis=1)
    h = _merge_sample_call(jnp.transpose(o_lat, (1, 0, 2)), ob, jnp.transpose(ext, (1, 0, 2)), x, merge_w, past + 1)
    kr_r, ki_r = _aux_split(aux[0])
    state = (ckv_n[0][:, None], kr_r[:, None], kb[0].reshape(n, 1, H_B, DH_B), vb[0].reshape(n, 1, H_B, DH_B),
             ki_r[:, None], ext[:, HALO - POOL_BUF:])
    return h, state


def _pick_tile(n, pref):
    t = min(n, pref)
    while n % t:
        t //= 2
    return t


def kernel(x_prompt, x_sample, cache_a_ckv, cache_a_krope, cache_b_k, cache_b_v, cache_b_kidx, state_pool, page_table, w_in, a_q_norm, a_kv_norm, a_w_uq, a_w_uk, a_w_uv, c_w_pool, c_scale, w_out, ln1_g, ln1_b, ln2_g, ln2_b, ffn_w_gate, ffn_w_up, ffn_w_down, moe_router, moe_w_gate, moe_w_up, moe_w_down):
    bsz, seq, d = x_prompt.shape
    n_dec = x_sample.shape[0]
    depth = w_in.shape[0]
    n_pages = page_table.shape[1]
    past = n_pages * PAGE_SIZE
    caches = (cache_a_ckv, jnp.transpose(cache_a_krope, (0, 1, 3, 2)), jnp.transpose(cache_b_k, (0, 1, 3, 4, 2)),
              jnp.transpose(cache_b_v, (0, 1, 3, 4, 2)), jnp.transpose(cache_b_kidx, (0, 1, 3, 2)))
    tiles = dict(proj=_pick_tile(seq, 512), mla=_pick_tile(seq, 256), dsa=_pick_tile(seq, 256),
                 merge=_pick_tile(seq, 512))
    tabs_p = _rope_tables(jnp.arange(seq, dtype=I32))
    tabs_s = _rope_tables(jnp.full((n_dec,), past, I32))
    hp = x_prompt
    hs = x_sample.reshape(n_dec, d)
    new_p, new_s = [], []
    for l in range(depth):
        proj_w, merge_w = _prep_layer(w_in[l], a_q_norm[l], a_kv_norm[l], a_w_uq[l], a_w_uk[l], a_w_uv[l],
                                      c_w_pool[l], c_scale[l], w_out[l], ln1_g[l], ln1_b[l])
        hp, st_p = _mixers_prompt(hp, proj_w, merge_w, tabs_p, tiles)
        hs, st_s = _mixers_sample(hs, proj_w, merge_w, tabs_s, caches, state_pool[l], page_table, l)
        g2, b2 = ln2_g[l].reshape(1, -1), ln2_b[l].reshape(1, -1)
        hp2 = hp.reshape(bsz * seq, d)
        j = l // 2
        tm = _pick_tile(bsz * seq, 1024)
        if l % 2 == 0:
            wg, wu, wd = ffn_w_gate[j].astype(BF16), ffn_w_up[j].astype(BF16), ffn_w_down[j].astype(BF16)
            hp2 = _ffn_call(hp2, wg, wu, wd, g2, b2, tm, 512)
            hs = _ffn_call(hs, wg, wu, wd, g2, b2, n_dec, 512)
        else:
            wg, wu, wd = moe_w_gate[j].astype(BF16), moe_w_up[j].astype(BF16), moe_w_down[j].astype(BF16)
            router = jnp.pad(moe_router[j], ((0, 0), (0, LANES - N_EXPERTS)))
            y2 = _moe_routed_call(hp2, _router_call(hp2, router, tm), wg, wu, wd, _pick_tile(bsz * seq, MOE_ROW_TILE),
                                  MOE_FF_TILE)
            hp2 = _combine_call(hp2, y2, g2, b2, tm)
            hs = _moe_call(hs, _router_call(hs, router, n_dec), wg, wu, wd, g2, b2, n_dec, 512)
        hp = hp2.reshape(bsz, seq, d)
        new_p.append(st_p)
        new_s.append(st_s)
    outs_p = [jnp.stack(a) for a in zip(*new_p)]
    outs_s = [jnp.stack(a) for a in zip(*new_s)]
    return (hp, hs.reshape(n_dec, 1, d), *outs_p, *outs_s)
```

```python
import functools

import numpy as np
import jax
import jax.numpy as jnp
from jax import lax
from jax.experimental import pallas as pl
from jax.experimental.pallas import tpu as pltpu

F32 = jnp.float32
BF16 = jnp.bfloat16
I32 = jnp.int32

D_MODEL = 1024
DEPTH = 2
PAGE_SIZE = 128
H_A, NOPE_A, ROPE_A, V_A = 8, 64, 32, 64
Q_RANK, KV_RANK = 256, 256
MLA_SCALE = (NOPE_A + ROPE_A) ** -0.5
H_B, DH_B = 4, 64
H_IDX, D_IDX = 4, 64
TOPK_MAX = 256
POOL_WINDOWS = (2, 4, 8, 16)
POOL_CH = 64
C_WIDTH = len(POOL_WINDOWS) * POOL_CH
POOL_BUF = max(POOL_WINDOWS) - 1
A_WIDTH = H_A * V_A
B_WIDTH = H_B * DH_B
IN_SPLITS = (Q_RANK, KV_RANK, ROPE_A, B_WIDTH, B_WIDTH, B_WIDTH, H_IDX * D_IDX, D_IDX, H_IDX, C_WIDTH)
D_FF = 3584
N_EXPERTS = 8
ROPE_THETA = 10000.0
ALPHA = (2 * DEPTH) ** 0.25
LN_EPS = 1e-5
RMS_EPS = 1e-6

LANES = 128
HALO = 16
QC_W = KV_RANK + LANES
ROPE_LANE0 = 64
AUX_WI0 = 96
INT_MIN = -(2 ** 31)
LOG2E = 1.4426950408889634
ROUTE_LANE0 = N_EXPERTS
MOE_ROW_TILE = 512
MOE_FF_TILE = 896
STREAM_SLOTS = 4
VMEM_LIMIT = 56 * 1024 * 1024


def _cparams(sem):
    return pltpu.CompilerParams(dimension_semantics=sem, vmem_limit_bytes=VMEM_LIMIT)


def _rms(x, g):
    return x * lax.rsqrt(jnp.mean(x * x, -1, keepdims=True) + RMS_EPS) * g


def _layer_norm(x, g, b):
    mu = jnp.mean(x, -1, keepdims=True)
    xc = x - mu
    var = jnp.mean(xc * xc, -1, keepdims=True)
    return xc * lax.rsqrt(var + LN_EPS) * g + b


def _dot(a, b):
    return jnp.dot(a, b, preferred_element_type=F32)


def _dot_nt(a, b):
    return lax.dot_general(a, b, (((1,), (1,)), ((), ())), preferred_element_type=F32)


def _count(mask):
    return jnp.sum(jnp.where(mask, 1.0, 0.0), axis=-1, keepdims=True)


def _key_to_float(key):
    return lax.bitcast_convert_type(key ^ ((key >> 31) & jnp.int32(0x7FFFFFFF)), F32)


def _kth_to_float(key):
    return jnp.where(key == jnp.int32(INT_MIN), -jnp.inf, _key_to_float(key))


def _topk_mask(score, valid, kpos, k, idx_bits):
    score = jnp.where(valid, score, -jnp.inf)
    rows = score.shape[0]
    kf = jnp.float32(k)

    def value_step(i, t):
        cand = t + (jnp.int32(1) << (jnp.int32(31) - i))
        return jnp.where(_count(score >= _key_to_float(cand)) >= kf, cand, t)

    t = _kth_to_float(lax.fori_loop(0, 32, value_step, jnp.full((rows, 1), INT_MIN, I32)))
    gt = score > t
    eq = (score == t) & valid
    need = kf - _count(gt)

    def index_step(i, c):
        cand = c + (jnp.int32(1) << (jnp.int32(idx_bits - 1) - i))
        return jnp.where(_count(eq & (kpos < cand)) < need, cand, c)

    c = lax.fori_loop(0, idx_bits, index_step, jnp.zeros((rows, 1), I32))
    return valid & (gt | (eq & (kpos <= c)))


_G_CQ, _G_CKV, _G_QB, _G_QBR, _G_KB, _G_KBR, _G_VB, _G_QI, _G_QIR, _G_XC = [256 * i for i in range(10)]
_G_A = 2560
_G_AR = 2688
W1_COLS = 2816


def _proj_body(x_ref, w1_ref, wq_ref, wuk_ref, gq_ref, gkv_ref, c64_ref, s64_ref, ca_ref, sa_ref, cq_ref, sq_ref,
               qc_ref, kc_ref, ckv_ref, kb_ref, vb_ref, aux_ref, kbb_ref, vbb_ref, kib_ref, qbb_ref, qih_ref,
               xc_ref):
    x = x_ref[0].astype(BF16)

    def proj(lo, width):
        return _dot(x, w1_ref[:, lo:lo + width])

    c64 = c64_ref[...]
    s64 = s64_ref[...]
    c256 = jnp.concatenate([c64, c64], axis=-1)
    s256 = jnp.concatenate([s64, s64], axis=-1)

    ckv_n = _rms(proj(_G_CKV, 256), gkv_ref[...])
    ckv_ref[0] = ckv_n

    qb = proj(_G_QB, 256) * c256 + proj(_G_QBR, 256) * s256
    qbb_ref[0] = qb.astype(BF16)
    kb = proj(_G_KB, 256) * c256 + proj(_G_KBR, 256) * s256
    kb_ref[0] = kb
    kbb_ref[0] = kb.astype(BF16)
    vb = proj(_G_VB, 256)
    vb_ref[0] = vb
    vbb_ref[0] = vb.astype(BF16)
    qi = proj(_G_QI, 256) * c256 + proj(_G_QIR, 256) * s256
    for h in range(H_IDX):
        qih_ref[0, h] = qi[:, h * D_IDX:(h + 1) * D_IDX].astype(BF16)
    xc_ref[0] = proj(_G_XC, 256)

    aux = proj(_G_A, LANES) * ca_ref[...] + proj(_G_AR, LANES) * sa_ref[...]
    aux_ref[0] = aux
    kib_ref[0] = aux[:, :D_IDX].astype(BF16)
    lane = lax.broadcasted_iota(I32, aux.shape, 1)
    kr_pad = jnp.where((lane >= ROPE_LANE0) & (lane < ROPE_LANE0 + ROPE_A), aux, 0.0)
    kc_ref[0] = jnp.concatenate([ckv_n, kr_pad], axis=-1).astype(BF16)

    cqn = _rms(proj(_G_CQ, 256), gq_ref[...]).astype(BF16)
    cq_t = cq_ref[...]
    sq_t = sq_ref[...]
    for h in range(H_A):
        q_nope = _dot(cqn, wq_ref[:, h * LANES:(h + 1) * LANES]).astype(BF16)
        q_lat = _dot(q_nope, wuk_ref[h])
        q_rope = (_dot(cqn, wq_ref[:, (H_A + h) * LANES:(H_A + h + 1) * LANES]) * cq_t
                  + _dot(cqn, wq_ref[:, (2 * H_A + h) * LANES:(2 * H_A + h + 1) * LANES]) * sq_t)
        qc_ref[0, h] = jnp.concatenate([q_lat, q_rope], axis=-1).astype(BF16)


def _proj_call(x, wts, tabs, tm):
    b, t, d = x.shape
    grid = (b, t // tm)
    tok = lambda w, dt: jax.ShapeDtypeStruct((b, t, w), dt)
    out_shape = (
        jax.ShapeDtypeStruct((b, H_A, t, QC_W), BF16),
        tok(QC_W, BF16),
        tok(KV_RANK, F32),
        tok(B_WIDTH, F32), tok(B_WIDTH, F32),
        tok(LANES, F32),
        tok(B_WIDTH, BF16), tok(B_WIDTH, BF16),
        tok(D_IDX, BF16),
        tok(B_WIDTH, BF16),
        jax.ShapeDtypeStruct((b, H_IDX, t, D_IDX), BF16),
        tok(C_WIDTH, F32),
    )
    tokspec = lambda w: pl.BlockSpec((1, tm, w), lambda bi, i: (bi, i, 0))
    headspec = lambda hh, w: pl.BlockSpec((1, hh, tm, w), lambda bi, i: (bi, 0, i, 0))
    full2 = lambda a: pl.BlockSpec(a.shape, lambda bi, i: (0, 0))
    full3 = lambda a: pl.BlockSpec(a.shape, lambda bi, i: (0, 0, 0))
    tabspec = pl.BlockSpec((tm, LANES), lambda bi, i: (i, 0))
    in_specs = [tokspec(d), full2(wts['w1']), full2(wts['wq']), full3(wts['wuk']), full2(wts['gq']), full2(wts['gkv'])]
    in_specs += [tabspec] * 6
    out_specs = (headspec(H_A, QC_W), tokspec(QC_W), tokspec(KV_RANK), tokspec(B_WIDTH), tokspec(B_WIDTH),
                 tokspec(LANES), tokspec(B_WIDTH), tokspec(B_WIDTH), tokspec(D_IDX), tokspec(B_WIDTH),
                 headspec(H_IDX, D_IDX), tokspec(C_WIDTH))
    return pl.pallas_call(
        _proj_body, out_shape=out_shape, grid=grid, in_specs=in_specs, out_specs=out_specs,
        compiler_params=_cparams(("parallel", "parallel")), name="proj",
    )(x, wts['w1'], wts['wq'], wts['wuk'], wts['gq'], wts['gkv'], *tabs)


def _flash_init(m_ref, l_ref, acc_ref):
    m_ref[...] = jnp.full(m_ref.shape, -jnp.inf, F32)
    l_ref[...] = jnp.zeros(l_ref.shape, F32)
    acc_ref[...] = jnp.zeros(acc_ref.shape, F32)


def _lane_tile(x, width):
    return x if width == LANES else jnp.concatenate([x] * (width // LANES), axis=1)


def _lane_fold(x):
    out = x[:, :LANES]
    for i in range(1, x.shape[1] // LANES):
        out = out + x[:, i * LANES:(i + 1) * LANES]
    return out


def _flash_step(q, k, v, keep, m_ref, l_ref, acc_ref, scale):
    s = _dot_nt(q, k) * (scale * LOG2E)
    if keep is not None:
        s = jnp.where(keep, s, -jnp.inf)
    _flash_update(s, v, m_ref, l_ref, acc_ref)


def _flash_update(s, v, m_ref, l_ref, acc_ref):
    m_prev = m_ref[...]
    m_new = jnp.maximum(m_prev, jnp.max(s, axis=-1, keepdims=True))
    m_use = jnp.where(m_new == -jnp.inf, 0.0, m_new)
    alpha = jnp.exp2(m_prev - m_use)
    p = jnp.exp2(s - _lane_tile(m_use, s.shape[1]))
    l_ref[...] = alpha * l_ref[...] + _lane_fold(p)
    acc_ref[...] = _lane_tile(alpha, v.shape[1]) * acc_ref[...] + _dot(p.astype(BF16), v)
    m_ref[...] = m_new


def _flash_result(l_ref, acc_ref):
    return acc_ref[...] * (1.0 / jnp.sum(l_ref[...], axis=-1, keepdims=True))


def _mla_body(qc_ref, kc_ref, o_ref, m_ref, l_ref, acc_ref, *, tq):
    qi = pl.program_id(1)
    q = qc_ref[0].reshape(H_A * tq, QC_W)
    _flash_init(m_ref, l_ref, acc_ref)

    def step(j, keep):
        k = kc_ref[0, pl.ds(pl.multiple_of(j * tq, tq), tq), :]
        _flash_step(q, k, k[:, :KV_RANK], keep, m_ref, l_ref, acc_ref, MLA_SCALE)

    def pair_step(p, carry):
        k = kc_ref[0, pl.ds(pl.multiple_of(p * 2 * tq, 2 * tq), 2 * tq), :]
        _flash_step(q, k, k[:, :KV_RANK], None, m_ref, l_ref, acc_ref, MLA_SCALE)
        return carry

    lax.fori_loop(0, qi // 2, pair_step, 0)

    @pl.when(qi % 2 == 1)
    def _():
        step(qi - 1, None)

    row = lax.broadcasted_iota(I32, (H_A * tq, tq), 0)
    col = lax.broadcasted_iota(I32, (H_A * tq, tq), 1)
    step(qi, col <= (row & (tq - 1)))
    o_ref[0] = _flash_result(l_ref, acc_ref).reshape(H_A, tq, KV_RANK).astype(BF16)


def _mla_call(qc, kc, tq):
    b, _, t, _ = qc.shape
    return pl.pallas_call(
        functools.partial(_mla_body, tq=tq),
        out_shape=jax.ShapeDtypeStruct((b, H_A, t, KV_RANK), BF16),
        grid=(b, t // tq),
        in_specs=[pl.BlockSpec((1, H_A, tq, QC_W), lambda bi, i: (bi, 0, i, 0)),
                  pl.BlockSpec((1, t, QC_W), lambda bi, i: (bi, 0, 0))],
        out_specs=pl.BlockSpec((1, H_A, tq, KV_RANK), lambda bi, i: (bi, 0, i, 0)),
        scratch_shapes=[pltpu.VMEM((H_A * tq, LANES), F32), pltpu.VMEM((H_A * tq, LANES), F32),
                        pltpu.VMEM((H_A * tq, KV_RANK), F32)],
        compiler_params=_cparams(("parallel", "parallel")), name="mla_prompt",
    )(qc, kc)


def _head_rows(q):
    lane = lax.broadcasted_iota(I32, q.shape, 1)
    return jnp.concatenate([jnp.where((lane >= h * DH_B) & (lane < (h + 1) * DH_B), q, 0.0) for h in range(H_B)],
                           axis=0)


def _head_lanes(o, r):
    lane = lax.broadcasted_iota(I32, (r, o.shape[1]), 1)
    out = jnp.zeros((r, o.shape[1]), F32)
    for h in range(H_B):
        out = jnp.where((lane >= h * DH_B) & (lane < (h + 1) * DH_B), o[h * r:(h + 1) * r], out)
    return out


def _dsa_body(qih_ref, aux_ref, kib_ref, qbb_ref, kbb_ref, vbb_ref, ob_ref, sc_ref, t_ref, cut_ref, m_ref, l_ref,
              acc_ref, *, tq, ck, topk, idx_bits):
    qi = pl.program_id(1)
    nk = ((qi + 1) * tq + ck - 1) // ck
    q4 = qih_ref[0].reshape(H_IDX * tq, D_IDX)
    w_t = aux_ref[0].T[AUX_WI0:AUX_WI0 + H_IDX, :] * (H_IDX ** -0.5)
    krow = lax.broadcasted_iota(I32, (ck, tq), 0)
    qpos = qi * tq + lax.broadcasted_iota(I32, (ck, tq), 1)

    def kpos_of(c):
        return c * ck + krow

    def score_chunk(c, carry):
        kc = kib_ref[0, pl.ds(pl.multiple_of(c * ck, ck), ck), :]
        d_t = _dot_nt(kc, q4) * (D_IDX ** -0.5)
        sc = jnp.zeros((ck, tq), F32)
        for h in range(H_IDX):
            sc = sc + jnp.maximum(d_t[:, h * tq:(h + 1) * tq], 0.0) * w_t[h:h + 1, :]
        sc_ref[c] = jnp.where(kpos_of(c) <= qpos, sc, -jnp.inf)
        return carry

    lax.fori_loop(0, nk, score_chunk, 0)

    kf = jnp.float32(topk)

    def search(n):
        def count(pred):
            part = jnp.zeros((8, tq), F32)
            for c in range(n):
                part = part + jnp.sum(jnp.where(pred(c), 1.0, 0.0).reshape(ck // 8, 8, tq), axis=0)
            return jnp.sum(part, axis=0, keepdims=True)

        def value_step(i, key):
            cand = key + (jnp.int32(1) << (jnp.int32(31) - i))
            cand_f = _key_to_float(cand)
            return jnp.where(count(lambda c: sc_ref[c] >= cand_f) >= kf, cand, key)

        t = _kth_to_float(lax.fori_loop(0, 32, value_step, jnp.full((1, tq), INT_MIN, I32)))
        need = kf - count(lambda c: sc_ref[c] > t)

        def tie(c):
            return (sc_ref[c] == t) & (kpos_of(c) <= qpos)

        t_ref[0:1, :] = t
        cut_ref[0:1, :] = jnp.full((1, tq), 2 ** idx_bits, I32)
        surplus = jnp.max(count(tie) - need)

        @pl.when(surplus > 0.0)
        def _():
            def index_step(i, cut):
                cand = cut + (jnp.int32(1) << (jnp.int32(idx_bits - 1) - i))
                return jnp.where(count(lambda c: tie(c) & (kpos_of(c) < cand)) < need, cand, cut)

            cut_ref[0:1, :] = lax.fori_loop(0, idx_bits, index_step, jnp.zeros((1, tq), I32))

    t_ref[0:1, :] = jnp.full((1, tq), -jnp.inf, F32)
    cut_ref[0:1, :] = jnp.zeros((1, tq), I32)
    for n in range(1, sc_ref.shape[0] + 1):
        pl.when((nk == n) & ((qi + 1) * tq > topk))(functools.partial(search, n))
    t = t_ref[0:1, :]
    cut = cut_ref[0:1, :]

    qm = _head_rows(qbb_ref[0].astype(F32)).astype(BF16)
    _flash_init(m_ref, l_ref, acc_ref)

    def attend(c, carry):
        sc = sc_ref[c]
        sel_t = (sc > t) | ((sc == t) & (kpos_of(c) <= qpos) & (kpos_of(c) <= cut))
        keep = jnp.where(sel_t, 1.0, 0.0).T
        keep = jnp.concatenate([keep] * H_B, axis=0) > 0.5
        rows = pl.ds(pl.multiple_of(c * ck, ck), ck)
        _flash_step(qm, kbb_ref[0, rows, :], vbb_ref[0, rows, :], keep, m_ref, l_ref, acc_ref, DH_B ** -0.5)
        return carry

    lax.fori_loop(0, nk, attend, 0)
    ob_ref[0] = _head_lanes(_flash_result(l_ref, acc_ref), tq).astype(BF16)


def _dsa_call(qih, aux, kib, qbb, kbb, vbb, tq, topk):
    b, t, _ = qbb.shape
    ck = _pick_tile(t, 256)
    idx_bits = max(1, int(np.ceil(np.log2(t))))
    whole = lambda w: pl.BlockSpec((1, t, w), lambda bi, i: (bi, 0, 0))
    tile = lambda w: pl.BlockSpec((1, tq, w), lambda bi, i: (bi, i, 0))
    return pl.pallas_call(
        functools.partial(_dsa_body, tq=tq, ck=ck, topk=topk, idx_bits=idx_bits),
        out_shape=jax.ShapeDtypeStruct((b, t, B_WIDTH), BF16),
        grid=(b, t // tq),
        in_specs=[pl.BlockSpec((1, H_IDX, tq, D_IDX), lambda bi, i: (bi, 0, i, 0)), tile(LANES), whole(D_IDX),
                  tile(B_WIDTH), whole(B_WIDTH), whole(B_WIDTH)],
        out_specs=tile(B_WIDTH),
        scratch_shapes=[pltpu.VMEM((t // ck, ck, tq), F32), pltpu.VMEM((8, tq), F32), pltpu.VMEM((8, tq), I32),
                        pltpu.VMEM((H_B * tq, LANES), F32),
                        pltpu.VMEM((H_B * tq, LANES), F32), pltpu.VMEM((H_B * tq, B_WIDTH), F32)],
        compiler_params=_cparams(("parallel", "parallel")), name="dsa_prompt",
    )(qih, aux, kib, qbb, kbb, vbb)


def _pool_delta(win_sums, cnts, xcur):
    lane = lax.broadcasted_iota(I32, xcur.shape, 1)
    mean = jnp.zeros(xcur.shape, F32)
    for g, w in enumerate(POOL_WINDOWS):
        in_group = (lane >= g * POOL_CH) & (lane < (g + 1) * POOL_CH)
        mean = jnp.where(in_group, win_sums[w] / cnts[w], mean)
    return mean - xcur


def _merge_tail(o_lat_heads, ob, delta, x, wuv_ref, woa_ref, wob_ref, woc_ref, wp_ref, cs_ref, g_ref, b_ref):
    oc = _dot(delta.astype(BF16), wp_ref[...]) * cs_ref[...]
    acc = _dot(ob, wob_ref[...]) + _dot(oc.astype(BF16), woc_ref[...])
    for h in range(H_A):
        oa = _dot(o_lat_heads(h), wuv_ref[h]).astype(BF16)
        acc = acc + _dot(oa, woa_ref[h])
    return _layer_norm(ALPHA * x + acc, g_ref[...], b_ref[...])


def _merge_body(ol_ref, ob_ref, xc_ref, buf_ref, x_ref, wuv_ref, woa_ref, wob_ref, woc_ref, wp_ref, cs_ref, g_ref,
                b_ref, h_ref, ext_ref, *, tm):
    i = pl.program_id(1)
    start = pl.multiple_of(i * tm, tm)

    @pl.when(i == 0)
    def _():
        ext_ref[0:HALO, :] = buf_ref[0]

    @pl.when(i > 0)
    def _():
        ext_ref[0:HALO, :] = xc_ref[0, pl.ds(start - HALO, HALO), :]

    xcur = xc_ref[0, pl.ds(start, tm), :]
    ext_ref[HALO:HALO + tm, :] = xcur
    run = xcur
    win_sums = {}
    for k in range(1, max(POOL_WINDOWS)):
        run = run + ext_ref[HALO - k:HALO - k + tm, :]
        if k + 1 in POOL_WINDOWS:
            win_sums[k + 1] = run
    pos1 = (start + 1 + lax.broadcasted_iota(I32, (tm, 1), 0)).astype(F32)
    cnts = {w: jnp.minimum(jnp.float32(w), pos1) for w in POOL_WINDOWS}
    delta = _pool_delta(win_sums, cnts, xcur)
    h_ref[0] = _merge_tail(lambda h: ol_ref[0, h], ob_ref[0], delta, x_ref[0], wuv_ref, woa_ref, wob_ref, woc_ref,
                           wp_ref, cs_ref, g_ref, b_ref)


def _merge_call(o_lat, ob, xc, buf, x, mw, tm):
    b, t, d = x.shape
    tile = lambda w: pl.BlockSpec((1, tm, w), lambda bi, i: (bi, i, 0))
    full = lambda a: pl.BlockSpec(a.shape, lambda bi, i: (0,) * a.ndim)
    wnames = ('wuv', 'woa', 'wob', 'woc', 'wp', 'cs', 'g1', 'b1')
    return pl.pallas_call(
        functools.partial(_merge_body, tm=tm),
        out_shape=jax.ShapeDtypeStruct((b, t, d), F32),
        grid=(b, t // tm),
        in_specs=[pl.BlockSpec((1, H_A, tm, KV_RANK), lambda bi, i: (bi, 0, i, 0)), tile(B_WIDTH),
                  pl.BlockSpec((1, t, C_WIDTH), lambda bi, i: (bi, 0, 0)),
                  pl.BlockSpec((1, HALO, C_WIDTH), lambda bi, i: (bi, 0, 0)), tile(d)]
                 + [full(mw[n]) for n in wnames],
        out_specs=tile(d),
        scratch_shapes=[pltpu.VMEM((HALO + tm, C_WIDTH), F32)],
        compiler_params=_cparams(("parallel", "arbitrary")), name="merge_prompt",
    )(o_lat, ob, xc, buf, x, *[mw[n] for n in wnames])


def _merge_sample_body(ol_ref, ob_ref, ext_ref, x_ref, wuv_ref, woa_ref, wob_ref, woc_ref, wp_ref, cs_ref, g_ref,
                       b_ref, h_ref, *, n_seen):
    n = ext_ref.shape[1]
    xcur = ext_ref[HALO - 1]
    run = xcur
    win_sums = {}
    for k in range(1, max(POOL_WINDOWS)):
        run = run + ext_ref[HALO - 1 - k]
        if k + 1 in POOL_WINDOWS:
            win_sums[k + 1] = run
    cnts = {w: jnp.full((n, 1), min(w, n_seen), F32) for w in POOL_WINDOWS}
    delta = _pool_delta(win_sums, cnts, xcur)
    h_ref[...] = _merge_tail(lambda h: ol_ref[h], ob_ref[...], delta, x_ref[...], wuv_ref, woa_ref, wob_ref, woc_ref,
                             wp_ref, cs_ref, g_ref, b_ref)


def _merge_sample_call(o_lat_h, ob, ext, x, mw, n_seen):
    n, d = x.shape
    wnames = ('wuv', 'woa', 'wob', 'woc', 'wp', 'cs', 'g1', 'b1')
    return pl.pallas_call(
        functools.partial(_merge_sample_body, n_seen=n_seen),
        out_shape=jax.ShapeDtypeStruct((n, d), F32), name="merge_sample",
        compiler_params=pltpu.CompilerParams(vmem_limit_bytes=VMEM_LIMIT),
    )(o_lat_h, ob, ext, x, *[mw[n_] for n_ in wnames])


def _swiglu_partial(xb, wg, wu, wd):
    a = _dot(xb, wg)
    u = _dot(xb, wu)
    hm = (a / (1.0 + jnp.exp(-a))) * u
    return _dot(hm.astype(BF16), wd)


def _ffn_body(h_ref, wg_ref, wu_ref, wd_ref, g_ref, b_ref, o_ref, xb_ref, acc_ref):
    k = pl.program_id(1)

    @pl.when(k == 0)
    def _():
        xb_ref[...] = h_ref[...].astype(BF16)
        acc_ref[...] = jnp.zeros(acc_ref.shape, F32)

    acc_ref[...] += _swiglu_partial(xb_ref[...], wg_ref[...], wu_ref[...], wd_ref[...])

    @pl.when(k == pl.num_programs(1) - 1)
    def _():
        o_ref[...] = _layer_norm(ALPHA * h_ref[...] + acc_ref[...], g_ref[...], b_ref[...])


def _ffn_call(h, wg, wu, wd, g, bta, tm, tf):
    n, d = h.shape
    f = wg.shape[1]
    return pl.pallas_call(
        _ffn_body, out_shape=jax.ShapeDtypeStruct((n, d), F32), grid=(n // tm, f // tf),
        in_specs=[pl.BlockSpec((tm, d), lambda i, k: (i, 0)), pl.BlockSpec((d, tf), lambda i, k: (0, k)),
                  pl.BlockSpec((d, tf), lambda i, k: (0, k)), pl.BlockSpec((tf, d), lambda i, k: (k, 0)),
                  pl.BlockSpec((1, d), lambda i, k: (0, 0)), pl.BlockSpec((1, d), lambda i, k: (0, 0))],
        out_specs=pl.BlockSpec((tm, d), lambda i, k: (i, 0)),
        scratch_shapes=[pltpu.VMEM((tm, d), BF16), pltpu.VMEM((tm, d), F32)],
        compiler_params=_cparams(("parallel", "arbitrary")), name="ffn_dense",
    )(h, wg, wu, wd, g, bta)


def _router_body(h_ref, r_ref, gate_ref):
    logits = jnp.dot(h_ref[...], r_ref[...], precision=lax.Precision.HIGHEST, preferred_element_type=F32)
    lane = lax.broadcasted_iota(I32, logits.shape, 1).astype(F32)
    logits = jnp.where(lane < N_EXPERTS, logits, -jnp.inf)
    v1 = jnp.max(logits, axis=-1, keepdims=True)
    i1 = jnp.min(jnp.where(logits == v1, lane, float(LANES)), axis=-1, keepdims=True)
    rest = jnp.where(lane == i1, -jnp.inf, logits)
    v2 = jnp.max(rest, axis=-1, keepdims=True)
    i2 = jnp.min(jnp.where(rest == v2, lane, float(LANES)), axis=-1, keepdims=True)
    e2 = jnp.exp(v2 - v1)
    g1 = 1.0 / (1.0 + e2)
    g2 = e2 / (1.0 + e2)
    dense = jnp.where(lane == i1, g1, jnp.where(lane == i2, g2, 0.0))
    for off, val in enumerate((i1, i2, g1, g2)):
        dense = jnp.where(lane == float(ROUTE_LANE0 + off), val, dense)
    gate_ref[...] = dense


def _router_call(h, router_pad, tm):
    n, d = h.shape
    return pl.pallas_call(
        _router_body, out_shape=jax.ShapeDtypeStruct((n, LANES), F32), grid=(n // tm,),
        in_specs=[pl.BlockSpec((tm, d), lambda i: (i, 0)), pl.BlockSpec((d, LANES), lambda i: (0, 0))],
        out_specs=pl.BlockSpec((tm, LANES), lambda i: (i, 0)),
        compiler_params=_cparams(("parallel",)), name="moe_router",
    )(h, router_pad)


def _moe_body(h_ref, gate_ref, wg_ref, wu_ref, wd_ref, g_ref, b_ref, o_ref, xb_ref, acc_ref):
    e = pl.program_id(1)
    k = pl.program_id(2)

    @pl.when((e == 0) & (k == 0))
    def _():
        xb_ref[...] = h_ref[...].astype(BF16)
        acc_ref[...] = jnp.zeros(acc_ref.shape, F32)

    gates = gate_ref[...]
    lane = lax.broadcasted_iota(I32, gates.shape, 1)
    ge = jnp.sum(jnp.where(lane == e, gates, 0.0), axis=-1, keepdims=True)
    acc_ref[...] += ge * _swiglu_partial(xb_ref[...], wg_ref[0], wu_ref[0], wd_ref[0])

    @pl.when((e == pl.num_programs(1) - 1) & (k == pl.num_programs(2) - 1))
    def _():
        o_ref[...] = _layer_norm(ALPHA * h_ref[...] + acc_ref[...], g_ref[...], b_ref[...])


def _moe_call(h, gates, wg, wu, wd, g, bta, tm, tf):
    n, d = h.shape
    ne, _, f = wg.shape
    return pl.pallas_call(
        _moe_body, out_shape=jax.ShapeDtypeStruct((n, d), F32), grid=(n // tm, ne, f // tf),
        in_specs=[pl.BlockSpec((tm, d), lambda i, e, k: (i, 0)), pl.BlockSpec((tm, LANES), lambda i, e, k: (i, 0)),
                  pl.BlockSpec((1, d, tf), lambda i, e, k: (e, 0, k)),
                  pl.BlockSpec((1, d, tf), lambda i, e, k: (e, 0, k)),
                  pl.BlockSpec((1, tf, d), lambda i, e, k: (e, k, 0)),
                  pl.BlockSpec((1, d), lambda i, e, k: (0, 0)), pl.BlockSpec((1, d), lambda i, e, k: (0, 0))],
        out_specs=pl.BlockSpec((tm, d), lambda i, e, k: (i, 0)),
        scratch_shapes=[pltpu.VMEM((tm, d), BF16), pltpu.VMEM((tm, d), F32)],
        compiler_params=_cparams(("parallel", "arbitrary", "arbitrary")), name="ffn_moe",
    )(h, gates, wg, wu, wd, g, bta)


def _route_plan(route, tm):
    n = route.shape[0]
    ea = route[:, ROUTE_LANE0:ROUTE_LANE0 + 2].astype(I32).T.reshape(-1)
    ga = route[:, ROUTE_LANE0 + 2:ROUTE_LANE0 + 4].T.reshape(-1)
    onehot = (ea[:, None] == jnp.arange(N_EXPERTS, dtype=I32)[None, :]).astype(I32)
    csum = jnp.cumsum(onehot, axis=0)
    counts = csum[-1]
    rank = jnp.sum(onehot * csum, axis=1) - 1
    tiles_e = (counts + tm - 1) // tm
    tile_end = jnp.cumsum(tiles_e)
    tile_off = tile_end - tiles_e
    pos = tile_off[ea] * tm + rank
    n_tiles = (2 * n) // tm + N_EXPERTS
    pair = jnp.arange(2 * n, dtype=I32)
    pair_of = jnp.full((n_tiles * tm,), -1, I32).at[pos].set(pair)
    real = pair_of >= 0
    tok_of = jnp.where(real, pair_of % n, 0)
    slot_row = jnp.arange(n_tiles * tm, dtype=I32)
    spare = 2 * n + ((slot_row // tm) % 2) * tm + slot_row % tm
    dst_of = jnp.where(real, pair_of, spare)
    gate_of = jnp.where(real, ga[jnp.maximum(pair_of, 0)], 0.0)
    tile_ids = jnp.arange(n_tiles, dtype=I32)
    te = jnp.minimum(jnp.sum((tile_ids[:, None] >= tile_end[None, :]).astype(I32), axis=1), N_EXPERTS - 1)
    nv = jnp.clip(counts[te] - (tile_ids - tile_off[te]) * tm, 0, tm)
    return (tok_of.reshape(n_tiles, 1, tm), dst_of.reshape(n_tiles, 1, tm), gate_of.reshape(n_tiles * tm, 1),
            te.astype(I32), nv.astype(I32))


def _moe_routed_body(te_ref, nv_ref, tokc_ref, tokn_ref, dstp_ref, dstc_ref, gate_ref, h_hbm, wg_ref, wu_ref, wd_ref,
                     y_hbm, xbuf, xb_ref, acc_ref, ybuf, gsem, ssem, *, tm, rows_per_step):
    i = pl.program_id(0)
    k = pl.program_id(1)
    n_t = pl.num_programs(0)
    slot = i % 2
    other = 1 - slot

    def gather_copy(tok_ref, r, buf_slot):
        return pltpu.make_async_copy(h_hbm.at[pl.ds(tok_ref[0, 0, r], 1), :], xbuf.at[buf_slot, pl.ds(r, 1), :],
                                     gsem.at[buf_slot])

    def scatter_copy(dst_ref, r, buf_slot):
        return pltpu.make_async_copy(ybuf.at[buf_slot, pl.ds(r, 1), :], y_hbm.at[pl.ds(dst_ref[0, 0, r], 1), :],
                                     ssem.at[buf_slot])

    def start_rows(copy_of):
        def body(r, carry):
            copy_of(r).start()
            return carry
        lax.fori_loop(0, tm, body, 0, unroll=8)

    def wait_gather(buf_slot):
        pltpu.make_async_copy(h_hbm.at[pl.ds(0, tm), :], xbuf.at[buf_slot], gsem.at[buf_slot]).wait()

    def wait_scatter(buf_slot):
        pltpu.make_async_copy(ybuf.at[buf_slot], y_hbm.at[pl.ds(0, tm), :], ssem.at[buf_slot]).wait()

    @pl.when((i == 0) & (k == 0))
    def _():
        start_rows(lambda r: gather_copy(tokc_ref, r, 0))
        ybuf[1] = jnp.zeros(ybuf.shape[1:], F32)

    @pl.when(k == 0)
    def _():
        wait_gather(slot)
        xb_ref[...] = xbuf[slot].astype(BF16)
        acc_ref[...] = jnp.zeros(acc_ref.shape, F32)

    base = k * rows_per_step
    for j in range(rows_per_step):
        gather_copy(tokn_ref, base + j, other).start()
        scatter_copy(dstp_ref, base + j, other).start()

    @pl.when(nv_ref[i] > 0)
    def _():
        acc_ref[...] += _swiglu_partial(xb_ref[...], wg_ref[0], wu_ref[0], wd_ref[0])

    @pl.when(k == pl.num_programs(1) - 1)
    def _():
        @pl.when(i > 0)
        def _():
            wait_scatter(slot)

        ybuf[slot] = acc_ref[...] * gate_ref[...]

        @pl.when(i == n_t - 1)
        def _():
            start_rows(lambda r: scatter_copy(dstc_ref, r, slot))
            wait_scatter(other)
            wait_scatter(slot)
            wait_gather(other)


def _moe_routed_call(h, route, wg, wu, wd, tm, tf):
    n, d = h.shape
    f = wg.shape[2]
    n_k = f // tf
    assert f % tf == 0 and tm % n_k == 0, (f, tf, tm)
    tok_of, dst_of, gate_of, te, nv = _route_plan(route, tm)
    n_tiles = tok_of.shape[0]
    first_prev = (2 * n + 2 * tm + jnp.arange(tm, dtype=I32)).reshape(1, 1, tm)
    dst_prev = jnp.concatenate([first_prev, dst_of[:-1]], axis=0)
    smem_row = lambda fn: pl.BlockSpec((1, 1, tm), fn, memory_space=pltpu.SMEM)
    cur = lambda i, k, te_, nv_: (i, 0, 0)
    grid_spec = pltpu.PrefetchScalarGridSpec(
        num_scalar_prefetch=2, grid=(n_tiles, n_k),
        in_specs=[smem_row(cur),
                  smem_row(lambda i, k, te_, nv_: (jnp.minimum(i + 1, n_tiles - 1), 0, 0)),
                  smem_row(cur), smem_row(cur),
                  pl.BlockSpec((tm, 1), lambda i, k, te_, nv_: (i, 0)),
                  pl.BlockSpec(memory_space=pl.ANY),
                  pl.BlockSpec((1, d, tf), lambda i, k, te_, nv_: (te_[i], 0, k)),
                  pl.BlockSpec((1, d, tf), lambda i, k, te_, nv_: (te_[i], 0, k)),
                  pl.BlockSpec((1, tf, d), lambda i, k, te_, nv_: (te_[i], k, 0))],
        out_specs=pl.BlockSpec(memory_space=pl.ANY),
        scratch_shapes=[pltpu.VMEM((2, tm, d), F32), pltpu.VMEM((tm, d), BF16), pltpu.VMEM((tm, d), F32),
                        pltpu.VMEM((2, tm, d), F32), pltpu.SemaphoreType.DMA((2,)), pltpu.SemaphoreType.DMA((2,))])
    return pl.pallas_call(
        functools.partial(_moe_routed_body, tm=tm, rows_per_step=tm // n_k),
        out_shape=jax.ShapeDtypeStruct((2 * n + 3 * tm, d), F32), grid_spec=grid_spec,
        compiler_params=_cparams(("arbitrary", "arbitrary")), name="ffn_moe_routed",
    )(te, nv, tok_of, tok_of, dst_prev, dst_of, gate_of, h, wg, wu, wd)


def _combine_body(h_ref, y0_ref, y1_ref, g_ref, b_ref, o_ref):
    o_ref[...] = _layer_norm(ALPHA * h_ref[...] + (y0_ref[...] + y1_ref[...]), g_ref[...], b_ref[...])


def _combine_call(h, y2, g, bta, tm):
    n, d = h.shape
    return pl.pallas_call(
        _combine_body, out_shape=jax.ShapeDtypeStruct((n, d), F32), grid=(n // tm,),
        in_specs=[pl.BlockSpec((tm, d), lambda i: (i, 0)), pl.BlockSpec((tm, d), lambda i: (i, 0)),
                  pl.BlockSpec((tm, d), lambda i: (i + n // tm, 0)),
                  pl.BlockSpec((1, d), lambda i: (0, 0)), pl.BlockSpec((1, d), lambda i: (0, 0))],
        out_specs=pl.BlockSpec((tm, d), lambda i: (i, 0)),
        compiler_params=_cparams(("parallel",)), name="moe_combine",
    )(h, y2, y2, g, bta)


def _stream_page_groups(pt_ref, layer, group, streams, sems, compute):
    b = pl.program_id(0)
    n_seq = pl.num_programs(0)
    n_groups = pt_ref.shape[1] // group
    ahead = STREAM_SLOTS - 1

    def start(seq, g, slot):
        for j in range(group):
            page = pt_ref[seq, g * group + j]
            for a, (hbm, buf) in enumerate(streams):
                pltpu.make_async_copy(hbm.at[layer, page], buf.at[slot, j], sems.at[a, slot]).start()

    def wait(slot):
        for a, (hbm, buf) in enumerate(streams):
            pltpu.make_async_copy(hbm.at[layer, pl.ds(0, group)], buf.at[slot], sems.at[a, slot]).wait()

    @pl.when(b == 0)
    def _():
        for g0 in range(ahead):
            start(0, g0, g0)

    def body(g, carry):
        slot = g % STREAM_SLOTS
        wait(slot)
        nxt = g + ahead
        nxt_slot = nxt % STREAM_SLOTS

        @pl.when(nxt < n_groups)
        def _():
            start(b, nxt, nxt_slot)

        @pl.when((nxt >= n_groups) & (b + 1 < n_seq))
        def _():
            start(b + 1, nxt - n_groups, nxt_slot)

        compute(g, slot)
        return carry

    lax.fori_loop(0, n_groups, body, 0)


def _mla_idx_sample_body(pt_ref, q_ref, knew_ref, qi_ref, w_ref, ck_hbm, kr_hbm, ki_hbm, o_ref, sc_ref,
                         ckbuf, krbuf, kibuf, sems, m_ref, l_ref, acc_ref, *, layer, group):
    _flash_init(m_ref, l_ref, acc_ref)
    q = q_ref[0]
    q_rope = q[:, KV_RANK + ROPE_LANE0:KV_RANK + ROPE_LANE0 + ROPE_A]
    w = w_ref[0] * (H_IDX ** -0.5)

    def compute(g, slot):
        ck = ckbuf[slot].reshape(group * PAGE_SIZE, KV_RANK).astype(BF16)
        kr_t = jnp.concatenate([krbuf[slot, j] for j in range(group)], axis=1).astype(BF16)
        s = (_dot_nt(q[:, :KV_RANK], ck) + _dot(q_rope, kr_t)) * (MLA_SCALE * LOG2E)
        _flash_update(s, ck, m_ref, l_ref, acc_ref)
        ki_t = jnp.concatenate([kibuf[slot, j] for j in range(group)], axis=1).astype(BF16)
        dots = _dot(qi_ref[0], ki_t) * (D_IDX ** -0.5)
        sc_ref[0, pl.ds(g, 1), :] = jnp.sum(jnp.maximum(dots, 0.0) * w, axis=0, keepdims=True)

    _stream_page_groups(pt_ref, layer, group, ((ck_hbm, ckbuf), (kr_hbm, krbuf), (ki_hbm, kibuf)), sems, compute)
    kn = knew_ref[0].astype(F32)
    s_n = jnp.sum(q.astype(F32) * kn, axis=-1, keepdims=True) * (MLA_SCALE * LOG2E)
    m_old = m_ref[...]
    m_fin = jnp.maximum(m_old, s_n)
    a = jnp.exp2(m_old - m_fin)
    p_n = jnp.exp2(s_n - m_fin)[:, 0:1]
    l_fin = jnp.sum(a * l_ref[...], axis=-1, keepdims=True) + p_n
    acc = _lane_tile(a, KV_RANK) * acc_ref[...] + p_n * kn[:, :KV_RANK]
    o_ref[0] = (acc * (1.0 / l_fin)).astype(BF16)


def _page_group(n_pages, most):
    g = _pick_tile(n_pages, most)
    while g > 1 and (n_pages // g) % STREAM_SLOTS:
        g //= 2
    assert (n_pages // g) % STREAM_SLOTS == 0, n_pages
    return g


def _mla_idx_sample_call(page_table, q, knew, qi, wi, cache_ckv, cache_kr_t, cache_ki_t, layer):
    n, n_pages = page_table.shape
    group = _page_group(n_pages, 16)
    n_groups = n_pages // group
    per_seq = lambda *blk: pl.BlockSpec((1,) + blk, lambda bi, pt: (bi,) + (0,) * len(blk))
    any_spec = pl.BlockSpec(memory_space=pl.ANY)
    grid_spec = pltpu.PrefetchScalarGridSpec(
        num_scalar_prefetch=1, grid=(n,),
        in_specs=[per_seq(H_A, QC_W), per_seq(1, QC_W), per_seq(H_IDX, D_IDX), per_seq(H_IDX, 1),
                  any_spec, any_spec, any_spec],
        out_specs=(per_seq(H_A, KV_RANK), per_seq(n_groups, group * PAGE_SIZE)),
        scratch_shapes=[pltpu.VMEM((STREAM_SLOTS, group, PAGE_SIZE, KV_RANK), F32),
                        pltpu.VMEM((STREAM_SLOTS, group, ROPE_A, PAGE_SIZE), F32),
                        pltpu.VMEM((STREAM_SLOTS, group, D_IDX, PAGE_SIZE), F32),
                        pltpu.SemaphoreType.DMA((3, STREAM_SLOTS)),
                        pltpu.VMEM((H_A, LANES), F32), pltpu.VMEM((H_A, LANES), F32),
                        pltpu.VMEM((H_A, KV_RANK), F32)])
    o_lat, score = pl.pallas_call(
        functools.partial(_mla_idx_sample_body, layer=layer, group=group),
        out_shape=(jax.ShapeDtypeStruct((n, H_A, KV_RANK), BF16),
                   jax.ShapeDtypeStruct((n, n_groups, group * PAGE_SIZE), F32)),
        grid_spec=grid_spec, compiler_params=_cparams(("arbitrary",)), name="mla_idx_sample",
    )(page_table, q, knew, qi, wi, cache_ckv, cache_kr_t, cache_ki_t)
    return o_lat, score.reshape(n, n_pages * PAGE_SIZE)


def _select_sample_body(sp_ref, qi_ref, w_ref, kin_ref, sel_ref, *, past, topk, idx_bits):
    n = sp_ref.shape[0]
    kn = kin_ref[...].astype(F32)
    s_new = jnp.zeros((n, 1), F32)
    for h in range(H_IDX):
        d = jnp.sum(qi_ref[h].astype(F32) * kn, axis=-1, keepdims=True) * (D_IDX ** -0.5)
        s_new = s_new + jnp.maximum(d, 0.0) * (w_ref[:, h:h + 1] * (H_IDX ** -0.5))
    lane = lax.broadcasted_iota(I32, (n, LANES), 1)
    tail = jnp.where(lane == 0, s_new, 0.0)
    score = jnp.concatenate([sp_ref[...], tail], axis=-1)
    kpos = lax.broadcasted_iota(I32, score.shape, 1)
    sel_ref[...] = jnp.where(_topk_mask(score, kpos <= past, kpos, topk, idx_bits), 1.0, 0.0)


def _select_sample_call(score_past, qi, wi, ki_new, topk):
    n, past = score_past.shape
    idx_bits = int(np.ceil(np.log2(past + LANES)))
    return pl.pallas_call(
        functools.partial(_select_sample_body, past=past, topk=topk, idx_bits=idx_bits),
        out_shape=jax.ShapeDtypeStruct((n, past + LANES), F32),
        compiler_params=pltpu.CompilerParams(vmem_limit_bytes=VMEM_LIMIT), name="select_sample",
    )(score_past, qi, wi, ki_new)


def _dsa_sample_body(pt_ref, q_ref, sel_ref, seln_ref, kn_ref, vn_ref, k_hbm, v_hbm, o_ref, kbuf, vbuf, sems,
                     m_ref, l_ref, acc_ref, *, layer, group):
    _flash_init(m_ref, l_ref, acc_ref)
    qcol = q_ref[0]
    scale = (DH_B ** -0.5) * LOG2E

    def compute(g, slot):
        keep = sel_ref[0, pl.ds(g, 1), :] > 0.5
        for h in range(H_B):
            qh = qcol[h * DH_B:(h + 1) * DH_B]
            k_t = jnp.concatenate([kbuf[slot, j, h] for j in range(group)], axis=1)
            s = jnp.sum(k_t * qh, axis=0, keepdims=True) * scale
            s = jnp.where(keep, s, -jnp.inf)
            m_prev = m_ref[h]
            m_new = jnp.maximum(m_prev, jnp.max(s, axis=-1, keepdims=True))
            m_use = jnp.where(m_new == -jnp.inf, 0.0, m_new)
            alpha = jnp.exp2(m_prev - m_use)
            p = jnp.exp2(s - _lane_tile(m_use, s.shape[1]))
            v_t = jnp.concatenate([vbuf[slot, j, h] for j in range(group)], axis=1)
            l_ref[h] = alpha * l_ref[h] + _lane_fold(p)
            acc_ref[h] = alpha * acc_ref[h] + _lane_fold(v_t * p)
            m_ref[h] = m_new

    _stream_page_groups(pt_ref, layer, group, ((k_hbm, kbuf), (v_hbm, vbuf)), sems, compute)
    new_kept = seln_ref[0][:, 0:1] > 0.5
    for h in range(H_B):
        rows = slice(h * DH_B, (h + 1) * DH_B)
        s_n = jnp.sum(kn_ref[0][rows] * qcol[rows], axis=0, keepdims=True) * scale
        s_n = jnp.where(new_kept, s_n, -jnp.inf)
        m_prev = m_ref[h]
        m_new = jnp.maximum(m_prev, s_n)
        m_use = jnp.where(m_new == -jnp.inf, 0.0, m_new)
        alpha = jnp.exp2(m_prev - m_use)
        p_n = jnp.exp2(s_n - m_use)[:, 0:1]
        l_tot = jnp.sum(alpha * l_ref[h], axis=-1, keepdims=True) + p_n
        acc = jnp.sum(alpha * acc_ref[h], axis=-1, keepdims=True) + p_n * vn_ref[0][rows]
        o_ref[0, rows, :] = acc * (1.0 / l_tot)


def _dsa_sample_call(page_table, q_col, sel, kn_col, vn_col, cache_k_t, cache_v_t, layer):
    n, n_pages = page_table.shape
    group = _page_group(n_pages, 8)
    n_groups = n_pages // group
    past = n_pages * PAGE_SIZE
    sel_past = sel[:, :past].reshape(n, n_groups, group * PAGE_SIZE)
    sel_new = sel[:, past:].reshape(n, 1, LANES)
    per_seq = lambda *blk: pl.BlockSpec((1,) + blk, lambda bi, pt: (bi,) + (0,) * len(blk))
    any_spec = pl.BlockSpec(memory_space=pl.ANY)
    grid_spec = pltpu.PrefetchScalarGridSpec(
        num_scalar_prefetch=1, grid=(n,),
        in_specs=[per_seq(B_WIDTH, 1), per_seq(n_groups, group * PAGE_SIZE), per_seq(1, LANES), per_seq(B_WIDTH, 1),
                  per_seq(B_WIDTH, 1), any_spec, any_spec],
        out_specs=per_seq(B_WIDTH, 1),
        scratch_shapes=[pltpu.VMEM((STREAM_SLOTS, group, H_B, DH_B, PAGE_SIZE), F32),
                        pltpu.VMEM((STREAM_SLOTS, group, H_B, DH_B, PAGE_SIZE), F32),
                        pltpu.SemaphoreType.DMA((2, STREAM_SLOTS)),
                        pltpu.VMEM((H_B, 1, LANES), F32), pltpu.VMEM((H_B, 1, LANES), F32),
                        pltpu.VMEM((H_B, DH_B, LANES), F32)])
    return pl.pallas_call(
        functools.partial(_dsa_sample_body, layer=layer, group=group),
        out_shape=jax.ShapeDtypeStruct((n, B_WIDTH, 1), F32), grid_spec=grid_spec,
        compiler_params=_cparams(("arbitrary",)), name="dsa_sample",
    )(page_table, q_col, sel_past, sel_new, kn_col, vn_col, cache_k_t, cache_v_t)


def _rot_cols(w, head_dim):
    k, n = w.shape
    wh = w.reshape(k, n // head_dim, 2, head_dim // 2)
    return jnp.concatenate([-wh[:, :, 1], wh[:, :, 0]], axis=-1).reshape(k, n)


def _prep_layer(w_in, a_q_norm, a_kv_norm, a_w_uq, a_w_uk, a_w_uv, c_w_pool, c_scale, w_out, ln_g, ln_b):
    offs = np.cumsum((0,) + IN_SPLITS)
    cq, ckv, kr, qb, kb, vb, qi, ki, wi, xc = [w_in[:, offs[i]:offs[i + 1]] for i in range(len(IN_SPLITS))]
    d = w_in.shape[0]
    zeros = lambda n: jnp.zeros((d, n), F32)
    grp_a = jnp.concatenate([ki, kr, wi, zeros(LANES - D_IDX - ROPE_A - H_IDX)], axis=1)
    grp_ar = jnp.concatenate([_rot_cols(ki, D_IDX), _rot_cols(kr, ROPE_A), zeros(LANES - D_IDX - ROPE_A)], axis=1)
    w1 = jnp.concatenate([cq, ckv, qb, _rot_cols(qb, DH_B), kb, _rot_cols(kb, DH_B), vb, qi, _rot_cols(qi, D_IDX),
                          xc, grp_a, grp_ar], axis=1).astype(BF16)
    uq = a_w_uq.reshape(Q_RANK, H_A, NOPE_A + ROPE_A)
    nope = jnp.pad(uq[:, :, :NOPE_A], ((0, 0), (0, 0), (0, LANES - NOPE_A)))
    rope = uq[:, :, NOPE_A:]
    rope_rot = jnp.concatenate([-rope[..., ROPE_A // 2:], rope[..., :ROPE_A // 2]], axis=-1)
    pad_rope = lambda r: jnp.pad(r, ((0, 0), (0, 0), (ROPE_LANE0, LANES - ROPE_LANE0 - ROPE_A)))
    wq = jnp.concatenate([nope.reshape(Q_RANK, -1), pad_rope(rope).reshape(Q_RANK, -1),
                          pad_rope(rope_rot).reshape(Q_RANK, -1)], axis=1).astype(BF16)
    wuk = jnp.pad(jnp.transpose(a_w_uk, (1, 2, 0)), ((0, 0), (0, LANES - NOPE_A), (0, 0))).astype(BF16)
    proj_w = dict(w1=w1, wq=wq, wuk=wuk, gq=a_q_norm.reshape(1, -1), gkv=a_kv_norm.reshape(1, -1))
    wuv = jnp.pad(jnp.transpose(a_w_uv, (1, 0, 2)), ((0, 0), (0, 0), (0, LANES - V_A))).astype(BF16)
    woa = jnp.pad(w_out[:A_WIDTH].reshape(H_A, V_A, -1), ((0, 0), (0, LANES - V_A), (0, 0))).astype(BF16)
    wp = jnp.zeros((C_WIDTH, C_WIDTH), F32)
    for g in range(len(POOL_WINDOWS)):
        wp = wp.at[g * POOL_CH:(g + 1) * POOL_CH, g * POOL_CH:(g + 1) * POOL_CH].set(c_w_pool[g])
    merge_w = dict(wuv=wuv, woa=woa, wob=w_out[A_WIDTH:A_WIDTH + B_WIDTH].astype(BF16),
                   woc=w_out[A_WIDTH + B_WIDTH:].astype(BF16), wp=wp.astype(BF16), cs=c_scale.reshape(1, -1),
                   g1=ln_g.reshape(1, -1), b1=ln_b.reshape(1, -1))
    return proj_w, merge_w


def _rope_tables(pos):
    posf = pos.astype(F32)[:, None]

    def cs(dim):
        inv = ROPE_THETA ** (-jnp.arange(0, dim, 2, dtype=F32) / dim)
        ang = posf * inv[None, :]
        c, s = jnp.cos(ang), jnp.sin(ang)
        return jnp.concatenate([c, c], -1), jnp.concatenate([s, s], -1)

    c64, s64 = cs(DH_B)
    c32, s32 = cs(ROPE_A)
    t = pos.shape[0]
    z = lambda n: jnp.zeros((t, n), F32)
    c_pair, s_pair = jnp.concatenate([c64, c64], -1), jnp.concatenate([s64, s64], -1)
    c_a = jnp.concatenate([c64, c32, jnp.ones((t, H_IDX), F32), z(LANES - D_IDX - ROPE_A - H_IDX)], -1)
    s_a = jnp.concatenate([s64, s32, z(LANES - D_IDX - ROPE_A)], -1)
    c_q = jnp.concatenate([z(ROPE_LANE0), c32, z(LANES - ROPE_LANE0 - ROPE_A)], -1)
    s_q = jnp.concatenate([z(ROPE_LANE0), s32, z(LANES - ROPE_LANE0 - ROPE_A)], -1)
    return c_pair, s_pair, c_a, s_a, c_q, s_q


def _aux_split(aux):
    return aux[..., D_IDX:D_IDX + ROPE_A], aux[..., :D_IDX]


def _mixers_prompt(x, proj_w, merge_w, tabs, tiles):
    b, t, _ = x.shape
    (qc, kc, ckv_n, kb, vb, aux, kbb, vbb, kib, qbb, qih, xc) = _proj_call(x, proj_w, tabs, tiles['proj'])
    o_lat = _mla_call(qc, kc, tiles['mla'])
    ob = _dsa_call(qih, aux, kib, qbb, kbb, vbb, tiles['dsa'], min(TOPK_MAX, t // 4))
    buf = jnp.zeros((b, HALO, C_WIDTH), F32)
    h = _merge_call(o_lat, ob, xc, buf, x, merge_w, tiles['merge'])
    kr_r, ki_r = _aux_split(aux)
    state = (ckv_n, kr_r, kb.reshape(b, t, H_B, DH_B), vb.reshape(b, t, H_B, DH_B), ki_r, xc[:, t - POOL_BUF:])
    return h, state


def _mixers_sample(x, proj_w, merge_w, tabs, caches, state_pool, page_table, layer):
    n = x.shape[0]
    c_ckv, c_kr, c_k, c_v, c_kidx = caches
    (qc, kc, ckv_n, kb, vb, aux, kbb, vbb, kib, qbb, qih, xc) = _proj_call(x.reshape(1, n, -1), proj_w, tabs, n)
    q = jnp.transpose(qc[0], (1, 0, 2))
    qi = jnp.transpose(qih[0], (1, 0, 2))
    wi = aux[0][:, AUX_WI0:AUX_WI0 + H_IDX]
    o_lat, score_past = _mla_idx_sample_call(page_table, q, kc[0][:, None, :], qi, wi[:, :, None], c_ckv, c_kr,
                                             c_kidx, layer)
    past = score_past.shape[1]
    sel = _select_sample_call(score_past, qih[0], wi, kib[0], min(TOPK_MAX, (past + 1) // 4))
    col = lambda a: a.astype(F32)[:, :, None]
    ob = _dsa_sample_call(page_table, col(qbb[0]), sel, col(kb[0]), col(vb[0]), c_k, c_v,
                          layer)[:, :, 0].astype(BF16)
    ext = jnp.concatenate([state_pool, xc[0][:, None, :]], axis=1)
    h = _merge_sample_call(jnp.transpose(o_lat, (1, 0, 2)), ob, jnp.transpose(ext, (1, 0, 2)), x, merge_w, past + 1)
    kr_r, ki_r = _aux_split(aux[0])
    state = (ckv_n[0][:, None], kr_r[:, None], kb[0].reshape(n, 1, H_B, DH_B), vb[0].reshape(n, 1, H_B, DH_B),
             ki_r[:, None], ext[:, HALO - POOL_BUF:])
    return h, state


def _pick_tile(n, pref):
    t = min(n, pref)
    while n % t:
        t //= 2
    return t


def kernel(x_prompt, x_sample, cache_a_ckv, cache_a_krope, cache_b_k, cache_b_v, cache_b_kidx, state_pool, page_table, w_in, a_q_norm, a_kv_norm, a_w_uq, a_w_uk, a_w_uv, c_w_pool, c_scale, w_out, ln1_g, ln1_b, ln2_g, ln2_b, ffn_w_gate, ffn_w_up, ffn_w_down, moe_router, moe_w_gate, moe_w_up, moe_w_down):
    bsz, seq, d = x_prompt.shape
    n_dec = x_sample.shape[0]
    depth = w_in.shape[0]
    n_pages = page_table.shape[1]
    past = n_pages * PAGE_SIZE
    caches = (cache_a_ckv, jnp.transpose(cache_a_krope, (0, 1, 3, 2)), jnp.transpose(cache_b_k, (0, 1, 3, 4, 2)),
              jnp.transpose(cache_b_v, (0, 1, 3, 4, 2)), jnp.transpose(cache_b_kidx, (0, 1, 3, 2)))
    tiles = dict(proj=_pick_tile(seq, 512), mla=_pick_tile(seq, 256), dsa=_pick_tile(seq, 512),
                 merge=_pick_tile(seq, 512))
    tabs_p = _rope_tables(jnp.arange(seq, dtype=I32))
    tabs_s = _rope_tables(jnp.full((n_dec,), past, I32))
    hp = x_prompt
    hs = x_sample.reshape(n_dec, d)
    new_p, new_s = [], []
    for l in range(depth):
        proj_w, merge_w = _prep_layer(w_in[l], a_q_norm[l], a_kv_norm[l], a_w_uq[l], a_w_uk[l], a_w_uv[l],
                                      c_w_pool[l], c_scale[l], w_out[l], ln1_g[l], ln1_b[l])
        hp, st_p = _mixers_prompt(hp, proj_w, merge_w, tabs_p, tiles)
        hs, st_s = _mixers_sample(hs, proj_w, merge_w, tabs_s, caches, state_pool[l], page_table, l)
        g2, b2 = ln2_g[l].reshape(1, -1), ln2_b[l].reshape(1, -1)
        hp2 = hp.reshape(bsz * seq, d)
        j = l // 2
        tm = _pick_tile(bsz * seq, 1024)
        if l % 2 == 0:
            wg, wu, wd = ffn_w_gate[j].astype(BF16), ffn_w_up[j].astype(BF16), ffn_w_down[j].astype(BF16)
            hp2 = _ffn_call(hp2, wg, wu, wd, g2, b2, tm, 512)
            hs = _ffn_call(hs, wg, wu, wd, g2, b2, n_dec, 512)
        else:
            wg, wu, wd = moe_w_gate[j].astype(BF16), moe_w_up[j].astype(BF16), moe_w_down[j].astype(BF16)
            router = jnp.pad(moe_router[j], ((0, 0), (0, LANES - N_EXPERTS)))
            y2 = _moe_routed_call(hp2, _router_call(hp2, router, tm), wg, wu, wd, _pick_tile(bsz * seq, MOE_ROW_TILE),
                                  MOE_FF_TILE)
            hp2 = _combine_call(hp2, y2, g2, b2, tm)
            hs = _moe_call(hs, _router_call(hs, router, n_dec), wg, wu, wd, g2, b2, n_dec, 512)
        hp = hp2.reshape(bsz, seq, d)
        new_p.append(st_p)
        new_s.append(st_s)
    outs_p = [jnp.stack(a) for a in zip(*new_p)]
    outs_s = [jnp.stack(a) for a in zip(*new_s)]
    return (hp, hs.reshape(n_dec, 1, d), *outs_p, *outs_s)
```
